```python
import math
import jax, jax.numpy as jnp
from jax import lax
import numpy as np

D_MODEL = 1024
BATCH = 4
SEQ = 4096
DEPTH = 4
DEC_BATCH = 128
DEC_SEQ = 1
PAST_LEN = 8192
PAGE_SIZE = 128

SB_HEADS = 8
SB_KV_HEADS = 2
SB_HEAD_DIM = 64
SB_GROUP = SB_HEADS // SB_KV_HEADS
SB_WIDTH = SB_HEADS * SB_HEAD_DIM
MLA_HEADS = 8
MLA_Q_RANK = 256
MLA_KV_RANK = 128
MLA_NOPE_DIM = 64
MLA_ROPE_DIM = 32
MLA_V_DIM = 64
MLA_WIDTH = MLA_HEADS * MLA_V_DIM
ROPE_THETA = 10000.0
POOL_WINDOWS = (2, 4, 8, 16)
POOL_GROUPS = 4
POOL_WIDTH = 512
POOL_GROUP_DIM = POOL_WIDTH // POOL_GROUPS
POOL_STATE_LEN = 15
N_BRANCHES = 3
Q_BLOCK = 128
RMS_EPS = 1e-6

IN_WIDTHS = (SB_WIDTH, SB_KV_HEADS * SB_HEAD_DIM, SB_KV_HEADS * SB_HEAD_DIM, SB_WIDTH,
             MLA_Q_RANK, MLA_KV_RANK + MLA_ROPE_DIM, MLA_WIDTH,
             POOL_WIDTH, POOL_WIDTH, N_BRANCHES * D_MODEL)
IN_WIDTH = sum(IN_WIDTHS)

kernel_name = "stickbreak_mla_pool_gated_hybrid_step"


def rmsnorm(x, g):
    xf = x.astype(jnp.float32)
    y = xf * lax.rsqrt(jnp.mean(xf * xf, axis=-1, keepdims=True) + RMS_EPS)
    return (y * g.astype(jnp.float32)).astype(x.dtype)


def rope(x, pos):
    half = MLA_ROPE_DIM // 2
    inv = ROPE_THETA ** (-jnp.arange(half, dtype=jnp.float32) / half)
    ang = pos.astype(jnp.float32)[:, None] * inv[None, :]
    ang = ang.reshape((1, pos.shape[0]) + (1,) * (x.ndim - 3) + (half,))
    cos, sin = jnp.cos(ang), jnp.sin(ang)
    xf = x.astype(jnp.float32)
    x1, x2 = xf[..., :half], xf[..., half:]
    return jnp.concatenate([x1 * cos - x2 * sin, x2 * cos + x1 * sin], axis=-1).astype(x.dtype)


def split_columns(z):
    idx = np.cumsum(np.array(IN_WIDTHS))[:-1].tolist()
    return jnp.split(z, idx, axis=-1)


def over_query_blocks(attend, qs, q_pos):
    T = q_pos.shape[0]
    if T <= Q_BLOCK:
        return attend(qs, q_pos)
    n = -(-T // Q_BLOCK)
    pad = n * Q_BLOCK - T
    B = qs[0].shape[0]
    qs = tuple(jnp.pad(a, [(0, 0), (0, pad)] + [(0, 0)] * (a.ndim - 2)) for a in qs)
    qp = jnp.pad(q_pos, (0, pad), mode="edge").reshape(n, Q_BLOCK)
    blk = tuple(jnp.moveaxis(a.reshape((B, n, Q_BLOCK) + a.shape[2:]), 1, 0) for a in qs)
    out = lax.map(lambda args: attend(args[0], args[1]), (blk, qp))
    out = jnp.moveaxis(out, 0, 1).reshape((B, n * Q_BLOCK) + out.shape[3:])
    return out[:, :T]


def sb_attend(q, k, v, q_pos, k_pos):
    z = jnp.einsum("bqhgd,bkhd->bhgqk", q, k).astype(jnp.float32) / math.sqrt(SB_HEAD_DIM)
    mask = k_pos[None, :] < q_pos[:, None]
    log_keep = jnp.where(mask, jax.nn.log_sigmoid(-z), 0.0)
    rc = lax.cumsum(log_keep, axis=log_keep.ndim - 1, reverse=True)
    after = jnp.concatenate([rc[..., 1:], jnp.zeros_like(rc[..., :1])], axis=-1)
    w = jnp.where(mask, jnp.exp(jax.nn.log_sigmoid(z) + after), 0.0)
    return jnp.einsum("bhgqk,bkhd->bqhgd", w.astype(v.dtype), v)


def mla_attend(q_lat, q_rope, ckv, kr, q_pos, k_pos):
    s = (jnp.einsum("bqhc,bkc->bhqk", q_lat, ckv) + jnp.einsum("bqhr,bkr->bhqk", q_rope, kr)).astype(jnp.float32)
    s = s / math.sqrt(MLA_NOPE_DIM + MLA_ROPE_DIM)
    s = jnp.where(k_pos[None, :] <= q_pos[:, None], s, -jnp.inf)
    p = jax.nn.softmax(s, axis=-1)
    return jnp.einsum("bhqk,bkc->bqhc", p.astype(ckv.dtype), ckv)


def pool_mix(u_ext, prefix, pos, w, scale):
    B, L, W = u_ext.shape
    T = pos.shape[0]
    uf = u_ext.astype(jnp.float32)
    cs = jnp.concatenate([jnp.zeros((B, 1, W), jnp.float32), jnp.cumsum(uf, axis=1)], axis=1)
    win = jnp.repeat(jnp.asarray(POOL_WINDOWS, jnp.int32), POOL_GROUP_DIM)
    end = prefix + 1 + jnp.arange(T, dtype=jnp.int32)
    start = jnp.maximum(end[:, None] - win[None, :], 0)
    s_end = cs[:, end]
    s_start = jnp.take_along_axis(cs, jnp.broadcast_to(start[None], (B, T, W)), axis=1)
    count = jnp.minimum(pos[:, None] + 1, win[None, :]).astype(jnp.float32)
    d = ((s_end - s_start) / count - uf[:, prefix:]).reshape(B, T, POOL_GROUPS, POOL_GROUP_DIM)
    y = jnp.einsum("btgc,gcd->btgd", d, w.astype(jnp.float32)).reshape(B, T, W) * scale.astype(jnp.float32)
    return y.astype(u_ext.dtype)


def trunk_layer(x, pos, p, past_pos=None, past_k=None, past_v=None, past_ckv=None, past_kr=None, past_u=None):
    B, T, _ = x.shape
    h = rmsnorm(x, p["norm_pre"])
    sb_q, sb_k, sb_v, sb_g, c_q, dkv, mla_g, pool_u, pool_g, br_g = split_columns(h @ p["w_in"])
    q = sb_q.reshape(B, T, SB_KV_HEADS, SB_GROUP, SB_HEAD_DIM)
    k = sb_k.reshape(B, T, SB_KV_HEADS, SB_HEAD_DIM)
    v = sb_v.reshape(B, T, SB_KV_HEADS, SB_HEAD_DIM)
    qf = (rmsnorm(c_q, p["mla_q_norm"]) @ p["mla_w_uq"]).reshape(B, T, MLA_HEADS, MLA_NOPE_DIM + MLA_ROPE_DIM)
    q_rope = rope(qf[..., MLA_NOPE_DIM:], pos)
    q_lat = jnp.einsum("bthd,chd->bthc", qf[..., :MLA_NOPE_DIM], p["mla_w_uk"])
    ckv = rmsnorm(dkv[..., :MLA_KV_RANK], p["mla_kv_norm"])
    kr = rope(dkv[..., MLA_KV_RANK:], pos)
    if past_pos is None:
        k_all, v_all, ckv_all, kr_all, k_pos, u_ext, prefix = k, v, ckv, kr, pos, pool_u, 0
    else:
        k_all = jnp.concatenate([past_k, k], axis=1)
        v_all = jnp.concatenate([past_v, v], axis=1)
        ckv_all = jnp.concatenate([past_ckv, ckv], axis=1)
        kr_all = jnp.concatenate([past_kr, kr], axis=1)
        k_pos = jnp.concatenate([past_pos, pos])
        u_ext = jnp.concatenate([past_u, pool_u], axis=1)
        prefix = past_u.shape[1]
    sb_o = over_query_blocks(lambda qs, qp: sb_attend(qs[0], k_all, v_all, qp, k_pos), (q,), pos)
    mla_lat = over_query_blocks(lambda qs, qp: mla_attend(qs[0], qs[1], ckv_all, kr_all, qp, k_pos), (q_lat, q_rope), pos)
    mla_o = jnp.einsum("bthc,chd->bthd", mla_lat, p["mla_w_uv"])
    pool_o = pool_mix(u_ext, prefix, pos, p["pool_w"], p["pool_scale"])
    o_sb = sb_o.reshape(B, T, SB_WIDTH) * jax.nn.silu(sb_g)
    o_mla = mla_o.reshape(B, T, MLA_WIDTH) * jax.nn.silu(mla_g)
    o_pool = pool_o * jax.nn.silu(pool_g)
    g = jax.nn.sigmoid(br_g.astype(jnp.float32)).astype(x.dtype).reshape(B, T, N_BRANCHES, D_MODEL)
    merged = (g[..., 0, :] * (o_sb @ p["w_br_sb"]) + g[..., 1, :] * (o_mla @ p["w_br_mla"])
              + g[..., 2, :] * (o_pool @ p["w_br_pool"]))
    y = x + rmsnorm(merged @ p["w_out"], p["norm_post"])
    return y, (k, v, ckv, kr, u_ext[:, -POOL_STATE_LEN:])


def setup_inputs(seed: int = 0) -> dict:
    key = jax.random.key(seed)
    ks = jax.random.split(key, 24)
    n_pages = PAST_LEN // PAGE_SIZE
    n_used = DEC_BATCH * n_pages
    n_pool = n_used + n_used // 4

    def nrm(k, shape, scale=1.0):
        return jax.random.normal(k, shape, jnp.float32) * scale

    def gain(k, shape):
        return 1.0 + 0.1 * jax.random.normal(k, shape, jnp.float32)

    page_table = jax.random.permutation(ks[7], n_pool)[:n_used].reshape(DEC_BATCH, n_pages).astype(jnp.int32)
    return {
        "x_prompt": nrm(ks[0], (BATCH, SEQ, D_MODEL)),
        "x_sample": nrm(ks[1], (DEC_BATCH, DEC_SEQ, D_MODEL)),
        "cache_sb_k": nrm(ks[2], (DEPTH, n_pool, PAGE_SIZE, SB_KV_HEADS, SB_HEAD_DIM)),
        "cache_sb_v": nrm(ks[3], (DEPTH, n_pool, PAGE_SIZE, SB_KV_HEADS, SB_HEAD_DIM)),
        "cache_mla_ckv": nrm(ks[4], (DEPTH, n_pool, PAGE_SIZE, MLA_KV_RANK)),
        "cache_mla_krope": nrm(ks[5], (DEPTH, n_pool, PAGE_SIZE, MLA_ROPE_DIM)),
        "state_pool": nrm(ks[6], (DEPTH, DEC_BATCH, POOL_STATE_LEN, POOL_WIDTH)),
        "page_table": page_table,
        "norm_pre": gain(ks[8], (DEPTH, D_MODEL)),
        "norm_post": gain(ks[9], (DEPTH, D_MODEL)),
        "w_in": nrm(ks[10], (DEPTH, D_MODEL, IN_WIDTH), D_MODEL ** -0.5),
        "mla_q_norm": gain(ks[11], (DEPTH, MLA_Q_RANK)),
        "mla_kv_norm": gain(ks[12], (DEPTH, MLA_KV_RANK)),
        "mla_w_uq": nrm(ks[13], (DEPTH, MLA_Q_RANK, MLA_HEADS * (MLA_NOPE_DIM + MLA_ROPE_DIM)), MLA_Q_RANK ** -0.5),
        "mla_w_uk": nrm(ks[14], (DEPTH, MLA_KV_RANK, MLA_HEADS, MLA_NOPE_DIM), MLA_KV_RANK ** -0.5),
        "mla_w_uv": nrm(ks[15], (DEPTH, MLA_KV_RANK, MLA_HEADS, MLA_V_DIM), MLA_KV_RANK ** -0.5),
        "pool_w": nrm(ks[16], (DEPTH, POOL_GROUPS, POOL_GROUP_DIM, POOL_GROUP_DIM), POOL_GROUP_DIM ** -0.5),
        "pool_scale": gain(ks[17], (DEPTH, POOL_WIDTH)),
        "w_br_sb": nrm(ks[18], (DEPTH, SB_WIDTH, D_MODEL), SB_WIDTH ** -0.5),
        "w_br_mla": nrm(ks[19], (DEPTH, MLA_WIDTH, D_MODEL), MLA_WIDTH ** -0.5),
        "w_br_pool": nrm(ks[20], (DEPTH, POOL_WIDTH, D_MODEL), POOL_WIDTH ** -0.5),
        "w_out": nrm(ks[21], (DEPTH, D_MODEL, D_MODEL), D_MODEL ** -0.5),
    }


def reference(x_prompt, x_sample, cache_sb_k, cache_sb_v, cache_mla_ckv, cache_mla_krope, state_pool, page_table,
              norm_pre, norm_post, w_in, mla_q_norm, mla_kv_norm, mla_w_uq, mla_w_uk, mla_w_uv,
              pool_w, pool_scale, w_br_sb, w_br_mla, w_br_pool, w_out):
    S = x_prompt.shape[1]
    T = x_sample.shape[1]
    DB, n_pages = page_table.shape
    past = n_pages * PAGE_SIZE
    pos_p = jnp.arange(S, dtype=jnp.int32)
    past_pos = jnp.arange(past, dtype=jnp.int32)
    pos_s = past + jnp.arange(T, dtype=jnp.int32)

    def gather(pool_l):
        g = pool_l[page_table]
        return g.reshape((DB, past) + g.shape[3:])

    kp_l, vp_l, cp_l, rp_l, up_l = [], [], [], [], []
    ks_l, vs_l, cs_l, rs_l, us_l = [], [], [], [], []
    xp, xs = x_prompt, x_sample
    for l in range(DEPTH):
        p = {"norm_pre": norm_pre[l], "norm_post": norm_post[l], "w_in": w_in[l],
             "mla_q_norm": mla_q_norm[l], "mla_kv_norm": mla_kv_norm[l], "mla_w_uq": mla_w_uq[l],
             "mla_w_uk": mla_w_uk[l], "mla_w_uv": mla_w_uv[l], "pool_w": pool_w[l], "pool_scale": pool_scale[l],
             "w_br_sb": w_br_sb[l], "w_br_mla": w_br_mla[l], "w_br_pool": w_br_pool[l], "w_out": w_out[l]}
        xp, (kp, vp, cp, rp, up) = trunk_layer(xp, pos_p, p)
        xs, (k_s, v_s, c_s, r_s, u_s) = trunk_layer(
            xs, pos_s, p, past_pos, gather(cache_sb_k[l]), gather(cache_sb_v[l]),
            gather(cache_mla_ckv[l]), gather(cache_mla_krope[l]), state_pool[l])
        kp_l.append(kp); vp_l.append(vp); cp_l.append(cp); rp_l.append(rp); up_l.append(up)
        ks_l.append(k_s); vs_l.append(v_s); cs_l.append(c_s); rs_l.append(r_s); us_l.append(u_s)
    return (xp, xs,
            jnp.stack(kp_l), jnp.stack(vp_l), jnp.stack(cp_l), jnp.stack(rp_l), jnp.stack(up_l),
            jnp.stack(ks_l), jnp.stack(vs_l), jnp.stack(cs_l), jnp.stack(rs_l), jnp.stack(us_l))
```

```python
import functools
import math

import jax
import jax.numpy as jnp
from jax import lax
from jax.experimental import pallas as pl
from jax.experimental.pallas import tpu as pltpu

F32 = jnp.float32
BF16 = jnp.bfloat16

D_MODEL = 1024
PAGE_SIZE = 128
SB_HEADS = 8
SB_KV_HEADS = 2
SB_HEAD_DIM = 64
SB_GROUP = SB_HEADS // SB_KV_HEADS
SB_WIDTH = SB_HEADS * SB_HEAD_DIM
MLA_HEADS = 8
MLA_Q_RANK = 256
MLA_KV_RANK = 128
MLA_NOPE_DIM = 64
MLA_ROPE_DIM = 32
MLA_V_DIM = 64
MLA_WIDTH = MLA_HEADS * MLA_V_DIM
ROPE_THETA = 10000.0
POOL_WINDOWS = (2, 4, 8, 16)
POOL_WIDTH = 512
POOL_GROUP_DIM = 128
POOL_STATE_LEN = 15
RMS_EPS = 1e-6

LANES = 128
SLOT = 128
MLA_QW = 256
VMEM_LIMIT = 56 * 1024 * 1024

SB_SLOT_HEADS = tuple((s // 2) + SB_GROUP * (s % 2) for s in range(SB_HEADS))

C_BRG = 0
C_SBQ = C_BRG + 3 * D_MODEL
C_GSB = C_SBQ + SB_HEADS * SLOT
C_GMLA = C_GSB + SB_WIDTH
C_PU = C_GMLA + MLA_WIDTH
C_PG = C_PU + POOL_WIDTH
C_CQ = C_PG + POOL_WIDTH
C_K = C_CQ + MLA_Q_RANK
C_V = C_K + LANES
C_CKV = C_V + LANES
C_R = C_CKV + LANES
C_RS = C_R + LANES
IN_PAD = C_RS + LANES


def _dot(a, b):
    return jnp.dot(a, b, preferred_element_type=F32)


def _dot_nt(a, b):
    return lax.dot_general(a, b, (((1,), (1,)), ((), ())), preferred_element_type=F32)


def _sigmoid(x):
    return 1.0 / (1.0 + jnp.exp(-x))


def _rms(x, g):
    return x * lax.rsqrt(jnp.mean(x * x, axis=-1, keepdims=True) + RMS_EPS) * g


def _const_spec(shape):
    nd = len(shape)
    return pl.BlockSpec(shape, lambda *_: (0,) * nd, pipeline_mode=pl.Buffered(1))


def _inproj_kernel(*refs, tm, decode, decode_count):
    (x_ref, npre_ref, win_ref, qn_ref, wuqn_ref, wukbd_ref, wuqr_ref, wuqrs_ref, kvn_ref,
     cos_ref, sin_ref, poolw_ref, pscale_ref) = refs[:13]
    refs = refs[13:]
    if decode:
        state_ref, refs = refs[0], refs[1:]
    (k32_ref, v32_ref, ckv32_ref, kr32_ref, pst_ref, sbq_ref, kb_ref, vb_ref, qcat_ref, kcat_ref,
     gsb_ref, gmla_ref, opool_ref, brg_ref) = refs[:14]
    refs = refs[14:]

    x = x_ref[0]
    h = _rms(x, npre_ref[...]).astype(BF16)

    def proj(off, width):
        return _dot(h, win_ref[:, off:off + width])

    for c in range(0, 3 * D_MODEL, 512):
        brg_ref[0, :, c:c + 512] = _sigmoid(proj(C_BRG + c, 512)).astype(BF16)

    for c in range(0, SB_HEADS * SLOT, 512):
        sbq_ref[0, :, c:c + 512] = (proj(C_SBQ + c, 512) * (1.0 / math.sqrt(SB_HEAD_DIM))).astype(BF16)
    k = proj(C_K, LANES)
    v = proj(C_V, LANES)
    k32_ref[0] = k
    v32_ref[0] = v
    kb_ref[0] = k.astype(BF16)
    vb_ref[0] = v.astype(BF16)
    g = proj(C_GSB, SB_WIDTH)
    gsb_ref[0] = (g * _sigmoid(g)).astype(BF16)
    g = proj(C_GMLA, MLA_WIDTH)
    gmla_ref[0] = (g * _sigmoid(g)).astype(BF16)

    cos = cos_ref[...]
    sin = sin_ref[...]
    ckv = _rms(proj(C_CKV, LANES), kvn_ref[...])
    krot = proj(C_R, LANES) * cos + proj(C_RS, LANES) * sin
    ckv32_ref[0] = ckv
    kr32_ref[0] = krot[:, :MLA_ROPE_DIM]
    kcat_ref[0, :, :LANES] = ckv.astype(BF16)
    kcat_ref[0, :, LANES:] = krot.astype(BF16)

    cqn = _rms(proj(C_CQ, MLA_Q_RANK), qn_ref[...]).astype(BF16)
    qnope = _dot(cqn, wuqn_ref[...]).astype(BF16)
    qscale = 1.0 / math.sqrt(MLA_NOPE_DIM + MLA_ROPE_DIM)
    for hd in range(MLA_HEADS):
        lo = hd * LANES
        qlat = _dot(qnope, wukbd_ref[:, lo:lo + LANES])
        qrot = _dot(cqn, wuqr_ref[:, lo:lo + LANES]) * cos + _dot(cqn, wuqrs_ref[:, lo:lo + LANES]) * sin
        qcat_ref[0, :, hd * MLA_QW:hd * MLA_QW + LANES] = (qlat * qscale).astype(BF16)
        qcat_ref[0, :, hd * MLA_QW + LANES:(hd + 1) * MLA_QW] = (qrot * qscale).astype(BF16)

    u = proj(C_PU, POOL_WIDTH)
    sums = []
    if decode:
        pst_ref[0] = u
        run = u
        nxt = POOL_STATE_LEN - 1
        for gi, w in enumerate(POOL_WINDOWS):
            while nxt >= POOL_STATE_LEN - (w - 1):
                run = run + state_ref[nxt]
                nxt -= 1
            sums.append(run[:, gi * LANES:(gi + 1) * LANES] * (1.0 / decode_count[gi]))
    else:
        prev_ref = refs[0]
        i = pl.program_id(1)

        @pl.when(i == 0)
        def _():
            prev_ref[...] = jnp.zeros_like(prev_ref)

        ext = jnp.concatenate([prev_ref[...], u], axis=0)
        prev_ref[...] = u[tm - 16:, :]
        pst_ref[0] = u[tm - 16:, :]
        pos = i * tm + lax.broadcasted_iota(jnp.int32, (tm, 1), 0)
        run = ext
        span = 1
        for gi, w in enumerate(POOL_WINDOWS):
            while span < w:
                run = run + pltpu.roll(run, span, 0)
                span *= 2
            cnt = jnp.minimum(pos + 1, w).astype(F32)
            sums.append(run[16:, gi * LANES:(gi + 1) * LANES] / cnt)
    pg = proj(C_PG, POOL_WIDTH)
    pgate = pg * _sigmoid(pg)
    for gi in range(len(POOL_WINDOWS)):
        sl = slice(gi * LANES, (gi + 1) * LANES)
        d = (sums[gi] - u[:, sl]).astype(BF16)
        y = _dot(d, poolw_ref[gi]) * pscale_ref[:, sl]
        opool_ref[0, :, sl] = (y * pgate[:, sl]).astype(BF16)


def _inproj(x, lw, cos, sin, *, tm, state_t=None, decode_count=None):
    B, S, _ = x.shape
    decode = state_t is not None
    nt = S // tm
    row = lambda w: pl.BlockSpec((1, tm, w), lambda b, i: (b, i, 0))
    in_specs = [
        row(D_MODEL),
        _const_spec((1, D_MODEL)),
        _const_spec((D_MODEL, IN_PAD)),
        _const_spec((1, MLA_Q_RANK)),
        _const_spec((MLA_Q_RANK, MLA_HEADS * MLA_NOPE_DIM)),
        _const_spec((MLA_HEADS * MLA_NOPE_DIM, MLA_HEADS * LANES)),
        _const_spec((MLA_Q_RANK, MLA_HEADS * LANES)),
        _const_spec((MLA_Q_RANK, MLA_HEADS * LANES)),
        _const_spec((1, MLA_KV_RANK)),
        pl.BlockSpec((tm, LANES), lambda b, i: (i, 0)),
        pl.BlockSpec((tm, LANES), lambda b, i: (i, 0)),
        _const_spec((len(POOL_WINDOWS), POOL_GROUP_DIM, POOL_GROUP_DIM)),
        _const_spec((1, POOL_WIDTH)),
    ]
    args = [x, lw["npre"], lw["win"], lw["qn"], lw["wuqn"], lw["wukbd"], lw["wuqr"], lw["wuqrs"], lw["kvn"],
            cos, sin, lw["poolw"], lw["pscale"]]
    if decode:
        in_specs.append(_const_spec(state_t.shape))
        args.append(state_t)
        pst_shape, pst_spec = (B, S, POOL_WIDTH), row(POOL_WIDTH)
        scratch = []
    else:
        pst_shape = (B, 16, POOL_WIDTH)
        pst_spec = pl.BlockSpec((1, 16, POOL_WIDTH), lambda b, i: (b, 0, 0))
        scratch = [pltpu.VMEM((16, POOL_WIDTH), F32)]
    outs = [("k32", LANES, F32), ("v32", LANES, F32), ("ckv32", LANES, F32), ("kr32", MLA_ROPE_DIM, F32),
            ("pst", None, F32), ("sbq", SB_HEADS * SLOT, BF16), ("kb", LANES, BF16), ("vb", LANES, BF16),
            ("qcat", MLA_HEADS * MLA_QW, BF16), ("kcat", 2 * LANES, BF16), ("gsb", SB_WIDTH, BF16),
            ("gmla", MLA_WIDTH, BF16), ("opool", POOL_WIDTH, BF16), ("brg", 3 * D_MODEL, BF16)]
    out_shape, out_specs = [], []
    for name, w, dt in outs:
        if name == "pst":
            out_shape.append(jax.ShapeDtypeStruct(pst_shape, dt))
            out_specs.append(pst_spec)
        else:
            out_shape.append(jax.ShapeDtypeStruct((B, S, w), dt))
            out_specs.append(row(w))
    res = pl.pallas_call(
        functools.partial(_inproj_kernel, tm=tm, decode=decode, decode_count=decode_count),
        grid=(B, nt),
        in_specs=in_specs,
        out_specs=out_specs,
        out_shape=out_shape,
        scratch_shapes=scratch,
        compiler_params=pltpu.CompilerParams(dimension_semantics=("arbitrary", "arbitrary"),
                                             vmem_limit_bytes=VMEM_LIMIT),
        name="inproj_decode" if decode else "inproj_prompt",
    )(*args)
    return dict(zip([o[0] for o in outs], res))


def _log_keep(z):
    return -(jnp.maximum(z, 0.0) + jnp.log(1.0 + jnp.exp(-jnp.abs(z))))


def _suffix_sums(lk, u_tri):
    hi = lk.astype(BF16)
    lo = (lk - hi.astype(F32)).astype(BF16)
    return _dot(hi, u_tri) + _dot(lo, u_tri)


def _sb_weights(z, u_tri, carry, valid=None):
    lk = _log_keep(z)
    if valid is not None:
        lk = jnp.where(valid, lk, 0.0)
    incl = _suffix_sums(lk, u_tri)
    w = jnp.exp(z + incl + carry)
    if valid is not None:
        w = jnp.where(valid, w, 0.0)
    return w.astype(BF16), carry + incl[:, 0:1]


def _sb_block(q, kblk, vblk, u_tri, carry, valid=None):
    w, carry = _sb_weights(_dot_nt(q, kblk), u_tri, carry, valid)
    return _dot(w, vblk), carry


def _sb_prompt_kernel(q_ref, k_ref, v_ref, u_ref, o_ref, *, t):
    i = pl.program_id(2)
    u_tri = u_ref[...]
    r = lax.broadcasted_iota(jnp.int32, (t, t), 0)
    c = lax.broadcasted_iota(jnp.int32, (t, t), 1)
    strict = c < r
    lane = lax.broadcasted_iota(jnp.int32, (t, LANES), 1)
    outs = []
    for hh in range(2):
        q = q_ref[0, :, hh * SLOT:(hh + 1) * SLOT]
        start = pl.multiple_of(i * t, t)
        acc, carry = _sb_block(q, k_ref[0, pl.ds(start, t), :], v_ref[0, pl.ds(start, t), :], u_tri,
                               jnp.zeros((t, 1), F32), strict)

        def body(jj, st):
            acc, carry = st
            s0 = pl.multiple_of((i - 1 - jj) * t, t)
            a, carry = _sb_block(q, k_ref[0, pl.ds(s0, t), :], v_ref[0, pl.ds(s0, t), :], u_tri, carry)
            return acc + a, carry

        acc, _ = lax.fori_loop(0, i, body, (acc, carry))
        outs.append(acc)
    o_ref[0] = jnp.where(lane < SB_HEAD_DIM, outs[0], outs[1]).astype(o_ref.dtype)


def _tri(t):
    j = lax.broadcasted_iota(jnp.int32, (t, t), 0)
    k = lax.broadcasted_iota(jnp.int32, (t, t), 1)
    return (j >= k).astype(BF16)


def _sb_prompt(sbq, kb, vb, *, t):
    B, S, _ = sbq.shape
    nq = S // t
    return pl.pallas_call(
        functools.partial(_sb_prompt_kernel, t=t),
        grid=(B, SB_GROUP, nq),
        in_specs=[
            pl.BlockSpec((1, t, 2 * SLOT), lambda b, p, i: (b, i, p)),
            pl.BlockSpec((1, S, LANES), lambda b, p, i: (b, 0, 0)),
            pl.BlockSpec((1, S, LANES), lambda b, p, i: (b, 0, 0)),
            _const_spec((t, t)),
        ],
        out_specs=pl.BlockSpec((1, t, LANES), lambda b, p, i: (b, i, p)),
        out_shape=jax.ShapeDtypeStruct((B, S, SB_WIDTH), BF16),
        compiler_params=pltpu.CompilerParams(dimension_semantics=("arbitrary",) * 3,
                                             vmem_limit_bytes=VMEM_LIMIT),
        name="sb_prompt",
    )(sbq, kb, vb, _tri(t))


def _softmax_block(q, kc, m, l, acc, valid=None):
    s = _dot_nt(q, kc)
    if valid is not None:
        s = jnp.where(valid, s, -jnp.inf)
    m_new = jnp.maximum(m, jnp.max(s, axis=-1, keepdims=True))
    alpha = jnp.exp(m - m_new)
    p = jnp.exp(s - m_new)
    l = alpha * l + jnp.sum(p, axis=-1, keepdims=True)
    acc = alpha * acc + _dot(p.astype(BF16), kc[:, :MLA_KV_RANK])
    return m_new, l, acc


def _mla_prompt_kernel(q_ref, kc_ref, o_ref, *, t):
    i = pl.program_id(1)
    q = q_ref[0]
    r = lax.broadcasted_iota(jnp.int32, (t, t), 0)
    c = lax.broadcasted_iota(jnp.int32, (t, t), 1)
    start = pl.multiple_of(i * t, t)
    st = _softmax_block(q, kc_ref[0, pl.ds(start, t), :], jnp.full((t, 1), -jnp.inf, F32),
                        jnp.zeros((t, 1), F32), jnp.zeros((t, MLA_KV_RANK), F32), c <= r)

    def body(j, st):
        s0 = pl.multiple_of(j * t, t)
        return _softmax_block(q, kc_ref[0, pl.ds(s0, t), :], *st)

    m, l, acc = lax.fori_loop(0, i, body, st)
    o_ref[0] = (acc / l).astype(o_ref.dtype)


def _mla_prompt(qcat, kcat, *, t):
    B, S, _ = qcat.shape
    nq = S // t
    return pl.pallas_call(
        functools.partial(_mla_prompt_kernel, t=t),
        grid=(B, nq, MLA_HEADS),
        in_specs=[
            pl.BlockSpec((1, t, MLA_QW), lambda b, i, h: (b, i, h)),
            pl.BlockSpec((1, S, 2 * LANES), lambda b, i, h: (b, 0, 0)),
        ],
        out_specs=pl.BlockSpec((1, t, MLA_KV_RANK), lambda b, i, h: (b, i, h)),
        out_shape=jax.ShapeDtypeStruct((B, S, MLA_HEADS * MLA_KV_RANK), BF16),
        compiler_params=pltpu.CompilerParams(dimension_semantics=("arbitrary",) * 3,
                                             vmem_limit_bytes=VMEM_LIMIT),
        name="mla_prompt",
    )(qcat, kcat)


def _merge_kernel(x_ref, sbo_ref, gsb_ref, mlat_ref, gmla_ref, opool_ref, brg_ref,
                  wsb_ref, wuv_ref, wmla_ref, wpool_ref, wout_ref, npost_ref, y_ref):
    o_sb = (sbo_ref[...].astype(F32) * gsb_ref[...].astype(F32)).astype(BF16)
    merged = brg_ref[:, :D_MODEL].astype(F32) * _dot(o_sb, wsb_ref[...])
    mla_o = _dot(mlat_ref[...], wuv_ref[...])
    o_mla = (mla_o * gmla_ref[...].astype(F32)).astype(BF16)
    merged += brg_ref[:, D_MODEL:2 * D_MODEL].astype(F32) * _dot(o_mla, wmla_ref[...])
    merged += brg_ref[:, 2 * D_MODEL:].astype(F32) * _dot(opool_ref[...], wpool_ref[...])
    o = _dot(merged.astype(BF16), wout_ref[...])
    y_ref[...] = x_ref[...] + _rms(o, npost_ref[...])


def _merge(x2, sbo, gsb, mlat, gmla, opool, brg, lw, *, tm):
    M = x2.shape[0]
    row = lambda w: pl.BlockSpec((tm, w), lambda i: (i, 0))
    return pl.pallas_call(
        _merge_kernel,
        grid=(M // tm,),
        in_specs=[row(D_MODEL), row(SB_WIDTH), row(SB_WIDTH), row(MLA_HEADS * MLA_KV_RANK), row(MLA_WIDTH),
                  row(POOL_WIDTH), row(3 * D_MODEL),
                  _const_spec((SB_WIDTH, D_MODEL)), _const_spec((MLA_HEADS * MLA_KV_RANK, MLA_WIDTH)),
                  _const_spec((MLA_WIDTH, D_MODEL)), _const_spec((POOL_WIDTH, D_MODEL)),
                  _const_spec((D_MODEL, D_MODEL)), _const_spec((1, D_MODEL))],
        out_specs=row(D_MODEL),
        out_shape=jax.ShapeDtypeStruct((M, D_MODEL), F32),
        compiler_params=pltpu.CompilerParams(dimension_semantics=("arbitrary",),
                                             vmem_limit_bytes=VMEM_LIMIT),
        name="merge",
    )(x2, sbo, gsb, mlat, gmla, opool, brg, lw["wsb"], lw["wuvbd"], lw["wmla"], lw["wpool"], lw["wout"],
      lw["npost"])


DEC_BLOCK = 256
DEC_PAGES = 16


def _paged_kernel(pt_ref, q_ref, qc_ref, knew_ref, u_ref, ck_hbm, cv_hbm, cc_hbm, cr_hbm,
                  sbo_ref, mlat_ref,
                  kbuf, vbuf, cbuf, rbuf, sem, carry_ref, sacc_ref, m_ref, l_ref, macc_ref,
                  *, layer, n_chunks):
    b = pl.program_id(0)
    c = pl.program_id(1)
    nb = pl.num_programs(0)
    unit = b * n_chunks + c
    slot = unit % 2
    bufs = ((ck_hbm, kbuf), (cv_hbm, vbuf), (cc_hbm, cbuf), (cr_hbm, rbuf))

    def page_copy(a, page, p, sl):
        hbm, buf = bufs[a]
        return pltpu.make_async_copy(hbm.at[layer, page], buf.at[sl, p], sem.at[a, sl])

    def issue(bb, cc, sl):
        first = (n_chunks - 1 - cc) * DEC_PAGES

        def body(p, _):
            page = pt_ref[bb, first + p]
            for a in range(4):
                page_copy(a, page, p, sl).start()
            return 0

        lax.fori_loop(0, DEC_PAGES, body, 0)

    @pl.when(unit == 0)
    def _():
        issue(b, c, slot)

    @pl.when(unit + 1 < nb * n_chunks)
    def _():
        nxt = unit + 1
        issue(nxt // n_chunks, nxt % n_chunks, 1 - slot)

    def wait_body(p, _):
        for a in range(4):
            page_copy(a, 0, p, slot).wait()
        return 0

    lax.fori_loop(0, DEC_PAGES, wait_body, 0)

    q_sb = q_ref[0]
    q_lat = qc_ref[0, :, :MLA_KV_RANK]
    q_rope = qc_ref[0, :, MLA_KV_RANK:MLA_KV_RANK + MLA_ROPE_DIM]
    u_tri = u_ref[...]

    @pl.when(c == 0)
    def _():
        carry_ref[...] = jnp.zeros_like(carry_ref)
        sacc_ref[...] = jnp.zeros_like(sacc_ref)
        kn = knew_ref[0].astype(F32)
        s_new = jnp.sum(qc_ref[0].astype(F32) * kn, axis=-1, keepdims=True)
        m_ref[...] = jnp.broadcast_to(s_new, m_ref.shape)
        l_ref[...] = jnp.ones_like(l_ref)
        macc_ref[...] = jnp.broadcast_to(kn[:, :MLA_KV_RANK], macc_ref.shape)

    carry = carry_ref[:, 0:1]
    sacc = sacc_ref[...]
    m = m_ref[:, 0:1]
    l = l_ref[:, 0:1]
    macc = macc_ref[...]
    ppb = DEC_BLOCK // PAGE_SIZE
    for blk in reversed(range(DEC_PAGES // ppb)):
        pages = range(blk * ppb, (blk + 1) * ppb)
        lanes = [slice(n * PAGE_SIZE, (n + 1) * PAGE_SIZE) for n in range(ppb)]
        z = jnp.concatenate([_dot(q_sb, kbuf[slot, pg].astype(BF16)) for pg in pages], axis=1)
        w, carry = _sb_weights(z, u_tri, carry)
        for pg, ln in zip(pages, lanes):
            sacc = sacc + _dot_nt(w[:, ln], vbuf[slot, pg].astype(BF16))
        cbs = [cbuf[slot, pg].astype(BF16) for pg in pages]
        s = jnp.concatenate([_dot_nt(q_lat, cb) + _dot(q_rope, rbuf[slot, pg].astype(BF16))
                             for pg, cb in zip(pages, cbs)], axis=1)
        m_new = jnp.maximum(m, jnp.max(s, axis=-1, keepdims=True))
        alpha = jnp.exp(m - m_new)
        p = jnp.exp(s - m_new)
        l = alpha * l + jnp.sum(p, axis=-1, keepdims=True)
        p = p.astype(BF16)
        macc = alpha * macc
        for cb, ln in zip(cbs, lanes):
            macc = macc + _dot(p[:, ln], cb)
        m = m_new
    carry_ref[...] = jnp.broadcast_to(carry, carry_ref.shape)
    sacc_ref[...] = sacc
    m_ref[...] = jnp.broadcast_to(m, m_ref.shape)
    l_ref[...] = jnp.broadcast_to(l, l_ref.shape)
    macc_ref[...] = macc

    @pl.when(c == n_chunks - 1)
    def _():
        lane = lax.broadcasted_iota(jnp.int32, (SB_GROUP, LANES), 1)
        sbo_ref[0] = jnp.where(lane < SB_HEAD_DIM, sacc[:SB_GROUP], sacc[SB_GROUP:]).astype(sbo_ref.dtype)
        mlat_ref[0] = (macc / l).astype(mlat_ref.dtype)


def _paged_attention(page_table, q_sb, qcat, knew, ck, cv, cc, cr, *, layer):
    DB, n_pages = page_table.shape
    n_chunks = n_pages // DEC_PAGES
    grid_spec = pltpu.PrefetchScalarGridSpec(
        num_scalar_prefetch=1,
        grid=(DB, n_chunks),
        in_specs=[
            pl.BlockSpec((1, SB_HEADS, SLOT), lambda b, c, pt: (b, 0, 0)),
            pl.BlockSpec((1, MLA_HEADS, MLA_QW), lambda b, c, pt: (b, 0, 0)),
            pl.BlockSpec((1, 1, 2 * LANES), lambda b, c, pt: (b, 0, 0)),
            pl.BlockSpec((DEC_BLOCK, DEC_BLOCK), lambda b, c, pt: (0, 0)),
            pl.BlockSpec(memory_space=pl.ANY),
            pl.BlockSpec(memory_space=pl.ANY),
            pl.BlockSpec(memory_space=pl.ANY),
            pl.BlockSpec(memory_space=pl.ANY),
        ],
        out_specs=[
            pl.BlockSpec((1, SB_GROUP, LANES), lambda b, c, pt: (b, 0, 0)),
            pl.BlockSpec((1, MLA_HEADS, MLA_KV_RANK), lambda b, c, pt: (b, 0, 0)),
        ],
        scratch_shapes=[
            pltpu.VMEM((2, DEC_PAGES, LANES, PAGE_SIZE), F32),
            pltpu.VMEM((2, DEC_PAGES, LANES, PAGE_SIZE), F32),
            pltpu.VMEM((2, DEC_PAGES, PAGE_SIZE, MLA_KV_RANK), F32),
            pltpu.VMEM((2, DEC_PAGES, MLA_ROPE_DIM, PAGE_SIZE), F32),
            pltpu.SemaphoreType.DMA((4, 2)),
            pltpu.VMEM((SB_HEADS, LANES), F32),
            pltpu.VMEM((SB_HEADS, LANES), F32),
            pltpu.VMEM((MLA_HEADS, LANES), F32),
            pltpu.VMEM((MLA_HEADS, LANES), F32),
            pltpu.VMEM((MLA_HEADS, MLA_KV_RANK), F32),
        ],
    )
    return pl.pallas_call(
        functools.partial(_paged_kernel, layer=layer, n_chunks=n_chunks),
        grid_spec=grid_spec,
        out_shape=[jax.ShapeDtypeStruct((DB, SB_GROUP, LANES), BF16),
                   jax.ShapeDtypeStruct((DB, MLA_HEADS, MLA_KV_RANK), BF16)],
        compiler_params=pltpu.CompilerParams(dimension_semantics=("arbitrary", "arbitrary"),
                                             vmem_limit_bytes=VMEM_LIMIT),
        name="paged_attention",
    )(page_table, q_sb, qcat, knew, _tri(DEC_BLOCK), ck, cv, cc, cr)


def _block_diag(blocks):
    n = len(blocks)
    r, c = blocks[0].shape
    rows = []
    for i, blk in enumerate(blocks):
        rows.append(jnp.concatenate([jnp.zeros((r, c * i), blk.dtype), blk,
                                     jnp.zeros((r, c * (n - 1 - i)), blk.dtype)], axis=1))
    return jnp.concatenate(rows, axis=0)


def _rope_cols(w):
    half = MLA_ROPE_DIM // 2
    z = jnp.zeros((w.shape[0], LANES - MLA_ROPE_DIM), w.dtype)
    return jnp.concatenate([w, z], axis=1), jnp.concatenate([-w[:, half:], w[:, :half], z], axis=1)


def _prep_layer(l, norm_pre, norm_post, w_in, mla_q_norm, mla_kv_norm, mla_w_uq, mla_w_uk, mla_w_uv,
                pool_w, pool_scale, w_br_sb, w_br_mla, w_br_pool, w_out):
    w = w_in[l].astype(BF16)
    o = 0
    parts = {}
    for name, width in (("sbq", SB_WIDTH), ("k", LANES), ("v", LANES), ("gsb", SB_WIDTH), ("cq", MLA_Q_RANK),
                        ("ckv", MLA_KV_RANK), ("r", MLA_ROPE_DIM), ("gmla", MLA_WIDTH), ("pu", POOL_WIDTH),
                        ("pg", POOL_WIDTH), ("brg", 3 * D_MODEL)):
        parts[name] = w[:, o:o + width]
        o += width
    z64 = jnp.zeros((D_MODEL, SB_HEAD_DIM), BF16)
    sbq_cols, gsb_cols = [], []
    for s, hd in enumerate(SB_SLOT_HEADS):
        qh = parts["sbq"][:, hd * SB_HEAD_DIM:(hd + 1) * SB_HEAD_DIM]
        sbq_cols += [qh, z64] if s % 2 == 0 else [z64, qh]
        gsb_cols.append(parts["gsb"][:, hd * SB_HEAD_DIM:(hd + 1) * SB_HEAD_DIM])
    r_cols, rs_cols = _rope_cols(parts["r"])
    win = jnp.concatenate([parts["brg"]] + sbq_cols + gsb_cols +
                          [parts["gmla"], parts["pu"], parts["pg"], parts["cq"], parts["k"], parts["v"],
                           parts["ckv"], r_cols, rs_cols], axis=1)

    wuq = mla_w_uq[l].astype(BF16).reshape(MLA_Q_RANK, MLA_HEADS, MLA_NOPE_DIM + MLA_ROPE_DIM)
    wuqn = wuq[:, :, :MLA_NOPE_DIM].reshape(MLA_Q_RANK, MLA_HEADS * MLA_NOPE_DIM)
    rr = [_rope_cols(wuq[:, hd, MLA_NOPE_DIM:]) for hd in range(MLA_HEADS)]
    wuqr = jnp.concatenate([a for a, _ in rr], axis=1)
    wuqrs = jnp.concatenate([b for _, b in rr], axis=1)
    wuk = mla_w_uk[l].astype(BF16)
    wukbd = _block_diag([wuk[:, hd, :].T for hd in range(MLA_HEADS)])
    wuv = mla_w_uv[l].astype(BF16)
    wuvbd = _block_diag([wuv[:, hd, :] for hd in range(MLA_HEADS)])
    wsb = w_br_sb[l].astype(BF16)
    wsb = jnp.concatenate([wsb[hd * SB_HEAD_DIM:(hd + 1) * SB_HEAD_DIM] for hd in SB_SLOT_HEADS], axis=0)
    return dict(
        npre=norm_pre[l].reshape(1, D_MODEL), npost=norm_post[l].reshape(1, D_MODEL), win=win,
        qn=mla_q_norm[l].reshape(1, MLA_Q_RANK), kvn=mla_kv_norm[l].reshape(1, MLA_KV_RANK),
        wuqn=wuqn, wukbd=wukbd, wuqr=wuqr, wuqrs=wuqrs, wuvbd=wuvbd,
        poolw=pool_w[l].astype(BF16), pscale=pool_scale[l].reshape(1, POOL_WIDTH),
        wsb=wsb, wmla=w_br_mla[l].astype(BF16), wpool=w_br_pool[l].astype(BF16), wout=w_out[l].astype(BF16))


def _rope_tables(pos):
    half = MLA_ROPE_DIM // 2
    inv = ROPE_THETA ** (-jnp.arange(half, dtype=F32) / half)
    ang = pos.astype(F32)[:, None] * inv[None, :]
    z = jnp.zeros((pos.shape[0], LANES - MLA_ROPE_DIM), F32)
    cos, sin = jnp.cos(ang), jnp.sin(ang)
    return jnp.concatenate([cos, cos, z], axis=1), jnp.concatenate([sin, sin, z], axis=1)


def _prompt_layer(x, lw, cos, sin):
    B, S, _ = x.shape
    tm = min(256, S)
    t = min(256, S)
    a = _inproj(x, lw, cos, sin, tm=tm)
    sbo = _sb_prompt(a["sbq"], a["kb"], a["vb"], t=t)
    mlat = _mla_prompt(a["qcat"], a["kcat"], t=t)
    flat = lambda v: v.reshape(B * S, v.shape[-1])
    y = _merge(flat(x), flat(sbo), flat(a["gsb"]), flat(mlat), flat(a["gmla"]), flat(a["opool"]),
               flat(a["brg"]), lw, tm=tm)
    return y.reshape(B, S, D_MODEL), a


def _sample_layer(x, lw, cos, sin, state, page_table, ck, cv, cc, cr, layer, past):
    DB = x.shape[0]
    count = tuple(float(min(past + 1, w)) for w in POOL_WINDOWS)
    a = _inproj(x.reshape(1, DB, D_MODEL), lw, cos, sin, tm=DB, state_t=jnp.transpose(state, (1, 0, 2)),
                decode_count=count)
    q_sb = a["sbq"].reshape(DB, SB_GROUP, 2, SLOT).transpose(0, 2, 1, 3).reshape(DB, SB_HEADS, SLOT)
    sbo, mlat = _paged_attention(page_table, q_sb, a["qcat"].reshape(DB, MLA_HEADS, MLA_QW),
                                 a["kcat"].reshape(DB, 1, 2 * LANES), ck, cv, cc, cr, layer=layer)
    flat = lambda v: v.reshape(DB, v.shape[-1])
    y = _merge(x.reshape(DB, D_MODEL), sbo.reshape(DB, SB_WIDTH), flat(a["gsb"]),
               mlat.reshape(DB, MLA_HEADS * MLA_KV_RANK), flat(a["gmla"]), flat(a["opool"]), flat(a["brg"]),
               lw, tm=DB)
    return y.reshape(DB, 1, D_MODEL), a


def kernel(x_prompt, x_sample, cache_sb_k, cache_sb_v, cache_mla_ckv, cache_mla_krope, state_pool, page_table,
           norm_pre, norm_post, w_in, mla_q_norm, mla_kv_norm, mla_w_uq, mla_w_uk, mla_w_uv,
           pool_w, pool_scale, w_br_sb, w_br_mla, w_br_pool, w_out):
    B, S, _ = x_prompt.shape
    DB, T, _ = x_sample.shape
    assert T == 1
    depth = w_in.shape[0]
    n_pages = page_table.shape[1]
    past = n_pages * PAGE_SIZE
    n_pool = cache_sb_k.shape[1]
    ck = jnp.transpose(cache_sb_k, (0, 1, 3, 4, 2)).reshape(depth, n_pool, LANES, PAGE_SIZE)
    cv = jnp.transpose(cache_sb_v, (0, 1, 3, 4, 2)).reshape(depth, n_pool, LANES, PAGE_SIZE)
    cr = jnp.transpose(cache_mla_krope, (0, 1, 3, 2))

    cos_p, sin_p = _rope_tables(jnp.arange(S, dtype=jnp.int32))
    cos_s, sin_s = _rope_tables(jnp.full((DB,), past, dtype=jnp.int32))

    xp, xs = x_prompt, x_sample
    outs = {n: [] for n in ("kp", "vp", "cp", "rp", "up", "ks", "vs", "cs", "rs", "us")}
    for l in range(depth):
        lw = _prep_layer(l, norm_pre, norm_post, w_in, mla_q_norm, mla_kv_norm, mla_w_uq, mla_w_uk, mla_w_uv,
                         pool_w, pool_scale, w_br_sb, w_br_mla, w_br_pool, w_out)
        xp, a = _prompt_layer(xp, lw, cos_p, sin_p)
        outs["kp"].append(a["k32"].reshape(B, S, SB_KV_HEADS, SB_HEAD_DIM))
        outs["vp"].append(a["v32"].reshape(B, S, SB_KV_HEADS, SB_HEAD_DIM))
        outs["cp"].append(a["ckv32"])
        outs["rp"].append(a["kr32"])
        outs["up"].append(a["pst"][:, 16 - POOL_STATE_LEN:])
        xs, a = _sample_layer(xs, lw, cos_s, sin_s, state_pool[l], page_table, ck, cv, cache_mla_ckv, cr, l, past)
        outs["ks"].append(a["k32"].reshape(DB, 1, SB_KV_HEADS, SB_HEAD_DIM))
        outs["vs"].append(a["v32"].reshape(DB, 1, SB_KV_HEADS, SB_HEAD_DIM))
        outs["cs"].append(a["ckv32"].reshape(DB, 1, MLA_KV_RANK))
        outs["rs"].append(a["kr32"].reshape(DB, 1, MLA_ROPE_DIM))
        outs["us"].append(jnp.concatenate([state_pool[l][:, 1:], a["pst"].reshape(DB, 1, POOL_WIDTH)], axis=1))
    st = lambda n: jnp.stack(outs[n])
    return (xp, xs, st("kp"), st("vp"), st("cp"), st("rp"), st("up"),
            st("ks"), st("vs"), st("cs"), st("rs"), st("us"))
```

```python
import functools
import math

import jax
import jax.numpy as jnp
from jax import lax
from jax.experimental import pallas as pl
from jax.experimental.pallas import tpu as pltpu

F32 = jnp.float32
BF16 = jnp.bfloat16

D_MODEL = 1024
PAGE_SIZE = 128
SB_HEADS = 8
SB_KV_HEADS = 2
SB_HEAD_DIM = 64
SB_GROUP = SB_HEADS // SB_KV_HEADS
SB_WIDTH = SB_HEADS * SB_HEAD_DIM
MLA_HEADS = 8
MLA_Q_RANK = 256
MLA_KV_RANK = 128
MLA_NOPE_DIM = 64
MLA_ROPE_DIM = 32
MLA_V_DIM = 64
MLA_WIDTH = MLA_HEADS * MLA_V_DIM
ROPE_THETA = 10000.0
POOL_WINDOWS = (2, 4, 8, 16)
POOL_WIDTH = 512
POOL_GROUP_DIM = 128
POOL_STATE_LEN = 15
RMS_EPS = 1e-6
LOG2E = math.log2(math.e)

LANES = 128
SLOT = 128
MLA_QW = 256
VMEM_LIMIT = 56 * 1024 * 1024
ROW_TILE = 256
KEY_TILE = 256
SB_Q_TILE = 512

SB_SLOT_HEADS = tuple((s // 2) + SB_GROUP * (s % 2) for s in range(SB_HEADS))

C_BRG = 0
C_SBQ = C_BRG + 3 * D_MODEL
C_GSB = C_SBQ + SB_HEADS * SLOT
C_GMLA = C_GSB + SB_WIDTH
C_PU = C_GMLA + MLA_WIDTH
C_PG = C_PU + POOL_WIDTH
C_CQ = C_PG + POOL_WIDTH
C_K = C_CQ + MLA_Q_RANK
C_V = C_K + LANES
C_CKV = C_V + LANES
C_R = C_CKV + LANES
C_RS = C_R + LANES
IN_PAD = C_RS + LANES


def _dot(a, b):
    return jnp.dot(a, b, preferred_element_type=F32)


def _dot_nt(a, b):
    return lax.dot_general(a, b, (((1,), (1,)), ((), ())), preferred_element_type=F32)


def _sigmoid(x):
    return 1.0 / (1.0 + jnp.exp(-x))


def _rms(x, g):
    return x * lax.rsqrt(jnp.mean(x * x, axis=-1, keepdims=True) + RMS_EPS) * g


def _const_spec(shape):
    nd = len(shape)
    return pl.BlockSpec(shape, lambda *_: (0,) * nd, pipeline_mode=pl.Buffered(1))


def _inproj_kernel(*refs, tm, decode, decode_count):
    (x_ref, npre_ref, win_ref, qn_ref, wuqn_ref, wukbd_ref, wuqr_ref, wuqrs_ref, kvn_ref,
     cos_ref, sin_ref, poolw_ref, pscale_ref) = refs[:13]
    refs = refs[13:]
    if decode:
        state_ref, refs = refs[0], refs[1:]
    (k32_ref, v32_ref, ckv32_ref, kr32_ref, pst_ref, sbq_ref, kb_ref, vb_ref, qcat_ref, kcat_ref,
     gsb_ref, gmla_ref, opool_ref, brg_ref) = refs[:14]
    refs = refs[14:]

    x = x_ref[0]
    h = _rms(x, npre_ref[...]).astype(BF16)

    def proj(off, width):
        return _dot(h, win_ref[:, off:off + width])

    for c in range(0, 3 * D_MODEL, 512):
        brg_ref[0, :, c:c + 512] = _sigmoid(proj(C_BRG + c, 512)).astype(BF16)

    for c in range(0, SB_HEADS * SLOT, 512):
        sbq_ref[0, :, c:c + 512] = (proj(C_SBQ + c, 512) * (LOG2E / math.sqrt(SB_HEAD_DIM))).astype(BF16)
    k = proj(C_K, LANES)
    v = proj(C_V, LANES)
    k32_ref[0] = k
    v32_ref[0] = v
    kb_ref[0] = k.astype(BF16)
    vb_ref[0] = v.astype(BF16)
    g = proj(C_GSB, SB_WIDTH)
    gsb_ref[0] = (g * _sigmoid(g)).astype(BF16)
    g = proj(C_GMLA, MLA_WIDTH)
    gmla_ref[0] = (g * _sigmoid(g)).astype(BF16)

    cos = cos_ref[...]
    sin = sin_ref[...]
    ckv = _rms(proj(C_CKV, LANES), kvn_ref[...])
    krot = proj(C_R, LANES) * cos + proj(C_RS, LANES) * sin
    ckv32_ref[0] = ckv
    kr32_ref[0] = krot[:, :MLA_ROPE_DIM]
    kcat_ref[0, :, :LANES] = ckv.astype(BF16)
    kcat_ref[0, :, LANES:] = krot.astype(BF16)

    cqn = _rms(proj(C_CQ, MLA_Q_RANK), qn_ref[...]).astype(BF16)
    qnope = _dot(cqn, wuqn_ref[...]).astype(BF16)
    qscale = LOG2E / math.sqrt(MLA_NOPE_DIM + MLA_ROPE_DIM)
    for hd in range(MLA_HEADS):
        lo = hd * LANES
        qlat = _dot(qnope, wukbd_ref[:, lo:lo + LANES])
        qrot = _dot(cqn, wuqr_ref[:, lo:lo + LANES]) * cos + _dot(cqn, wuqrs_ref[:, lo:lo + LANES]) * sin
        qcat_ref[0, :, hd * MLA_QW:hd * MLA_QW + LANES] = (qlat * qscale).astype(BF16)
        qcat_ref[0, :, hd * MLA_QW + LANES:(hd + 1) * MLA_QW] = (qrot * qscale).astype(BF16)

    u = proj(C_PU, POOL_WIDTH)
    sums = []
    if decode:
        pst_ref[0] = u
        run = u
        nxt = POOL_STATE_LEN - 1
        for gi, w in enumerate(POOL_WINDOWS):
            while nxt >= POOL_STATE_LEN - (w - 1):
                run = run + state_ref[nxt]
                nxt -= 1
            sums.append(run[:, gi * LANES:(gi + 1) * LANES] * (1.0 / decode_count[gi]))
    else:
        prev_ref = refs[0]
        i = pl.program_id(1)

        @pl.when(i == 0)
        def _():
            prev_ref[...] = jnp.zeros_like(prev_ref)

        ext = jnp.concatenate([prev_ref[...], u], axis=0)
        prev_ref[...] = u[tm - 16:, :]
        pst_ref[0] = u[tm - 16:, :]
        pos = i * tm + lax.broadcasted_iota(jnp.int32, (tm, 1), 0)
        run = ext
        span = 1
        for gi, w in enumerate(POOL_WINDOWS):
            while span < w:
                run = run + pltpu.roll(run, span, 0)
                span *= 2
            cnt = jnp.minimum(pos + 1, w).astype(F32)
            sums.append(run[16:, gi * LANES:(gi + 1) * LANES] / cnt)
    pg = proj(C_PG, POOL_WIDTH)
    pgate = pg * _sigmoid(pg)
    for gi in range(len(POOL_WINDOWS)):
        sl = slice(gi * LANES, (gi + 1) * LANES)
        d = (sums[gi] - u[:, sl]).astype(BF16)
        y = _dot(d, poolw_ref[gi]) * pscale_ref[:, sl]
        opool_ref[0, :, sl] = (y * pgate[:, sl]).astype(BF16)


def _inproj(x, lw, cos, sin, *, tm, state_t=None, decode_count=None):
    B, S, _ = x.shape
    decode = state_t is not None
    nt = S // tm
    row = lambda w: pl.BlockSpec((1, tm, w), lambda b, i: (b, i, 0))
    in_specs = [
        row(D_MODEL),
        _const_spec((1, D_MODEL)),
        _const_spec((D_MODEL, IN_PAD)),
        _const_spec((1, MLA_Q_RANK)),
        _const_spec((MLA_Q_RANK, MLA_HEADS * MLA_NOPE_DIM)),
        _const_spec((MLA_HEADS * MLA_NOPE_DIM, MLA_HEADS * LANES)),
        _const_spec((MLA_Q_RANK, MLA_HEADS * LANES)),
        _const_spec((MLA_Q_RANK, MLA_HEADS * LANES)),
        _const_spec((1, MLA_KV_RANK)),
        pl.BlockSpec((tm, LANES), lambda b, i: (i, 0)),
        pl.BlockSpec((tm, LANES), lambda b, i: (i, 0)),
        _const_spec((len(POOL_WINDOWS), POOL_GROUP_DIM, POOL_GROUP_DIM)),
        _const_spec((1, POOL_WIDTH)),
    ]
    args = [x, lw["npre"], lw["win"], lw["qn"], lw["wuqn"], lw["wukbd"], lw["wuqr"], lw["wuqrs"], lw["kvn"],
            cos, sin, lw["poolw"], lw["pscale"]]
    if decode:
        in_specs.append(_const_spec(state_t.shape))
        args.append(state_t)
        pst_shape, pst_spec = (B, S, POOL_WIDTH), row(POOL_WIDTH)
        scratch = []
    else:
        pst_shape = (B, 16, POOL_WIDTH)
        pst_spec = pl.BlockSpec((1, 16, POOL_WIDTH), lambda b, i: (b, 0, 0))
        scratch = [pltpu.VMEM((16, POOL_WIDTH), F32)]
    outs = [("k32", LANES, F32), ("v32", LANES, F32), ("ckv32", LANES, F32), ("kr32", MLA_ROPE_DIM, F32),
            ("pst", None, F32), ("sbq", SB_HEADS * SLOT, BF16), ("kb", LANES, BF16), ("vb", LANES, BF16),
            ("qcat", MLA_HEADS * MLA_QW, BF16), ("kcat", 2 * LANES, BF16), ("gsb", SB_WIDTH, BF16),
            ("gmla", MLA_WIDTH, BF16), ("opool", POOL_WIDTH, BF16), ("brg", 3 * D_MODEL, BF16)]
    out_shape, out_specs = [], []
    for name, w, dt in outs:
        if name == "pst":
            out_shape.append(jax.ShapeDtypeStruct(pst_shape, dt))
            out_specs.append(pst_spec)
        else:
            out_shape.append(jax.ShapeDtypeStruct((B, S, w), dt))
            out_specs.append(row(w))
    res = pl.pallas_call(
        functools.partial(_inproj_kernel, tm=tm, decode=decode, decode_count=decode_count),
        grid=(B, nt),
        in_specs=in_specs,
        out_specs=out_specs,
        out_shape=out_shape,
        scratch_shapes=scratch,
        compiler_params=pltpu.CompilerParams(dimension_semantics=("arbitrary", "arbitrary"),
                                             vmem_limit_bytes=VMEM_LIMIT),
        name="inproj_decode" if decode else "inproj_prompt",
    )(*args)
    return dict(zip([o[0] for o in outs], res))


def _softplus2(z):
    sign = jnp.uint32(0x80000000)
    neg_abs = lax.bitcast_convert_type(lax.bitcast_convert_type(z, jnp.uint32) | sign, F32)
    return jnp.maximum(z, 0.0) + jnp.log(1.0 + jnp.exp2(neg_abs)) * LOG2E


def _suffix_sums(x, u_tri):
    return _dot(x.astype(BF16), u_tri)


def _sb_weights(z, u_tri, carry, valid=None):
    sp = _softplus2(z)
    if valid is not None:
        sp = jnp.where(valid, sp, 0.0)
    incl = _suffix_sums(sp, u_tri)
    w = jnp.exp2(z - incl - carry)
    if valid is not None:
        w = jnp.where(valid, w, 0.0)
    return w.astype(BF16), carry + incl[:, 0:1]


def _sb_block(q, kblk, vblk, u_tri, carry, valid=None):
    w, carry = _sb_weights(_dot_nt(q, kblk), u_tri, carry, valid)
    return _dot(w, vblk), carry


def _sb_prompt_kernel(q_ref, k_ref, v_ref, u_ref, o_ref, *, tq, tk):
    i = pl.program_id(2)
    u_tri = u_ref[...]
    nsub = tq // tk
    q = jnp.concatenate([q_ref[0, :, :SLOT], q_ref[0, :, SLOT:]], axis=0)
    rows = 2 * tq
    qpos = lax.broadcasted_iota(jnp.int32, (rows, tk), 0) & (tq - 1)
    kpos = lax.broadcasted_iota(jnp.int32, (rows, tk), 1)

    def block(start, carry, valid=None):
        start = pl.multiple_of(start, tk)
        return _sb_block(q, k_ref[0, pl.ds(start, tk), :], v_ref[0, pl.ds(start, tk), :], u_tri, carry, valid)

    acc = jnp.zeros((rows, LANES), F32)
    carry = jnp.zeros((rows, 1), F32)
    for sub in reversed(range(nsub)):
        a, carry = block(i * tq + sub * tk, carry, kpos + sub * tk < qpos)
        acc = acc + a

    def body(jj, st):
        acc, carry = st
        for sub in range(nsub):
            a, carry = block((i - jj) * tq - (sub + 1) * tk, carry)
            acc = acc + a
        return acc, carry

    acc, _ = lax.fori_loop(0, i, body, (acc, carry))
    lane = lax.broadcasted_iota(jnp.int32, (tq, LANES), 1)
    o_ref[0] = jnp.where(lane < SB_HEAD_DIM, acc[:tq], acc[tq:]).astype(o_ref.dtype)


def _tri(t):
    j = lax.broadcasted_iota(jnp.int32, (t, t), 0)
    k = lax.broadcasted_iota(jnp.int32, (t, t), 1)
    return (j >= k).astype(BF16)


def _sb_prompt(sbq, kb, vb, *, tq, tk):
    B, S, _ = sbq.shape
    return pl.pallas_call(
        functools.partial(_sb_prompt_kernel, tq=tq, tk=tk),
        grid=(B, SB_GROUP, S // tq),
        in_specs=[
            pl.BlockSpec((1, tq, 2 * SLOT), lambda b, p, i: (b, i, p)),
            pl.BlockSpec((1, S, LANES), lambda b, p, i: (b, 0, 0)),
            pl.BlockSpec((1, S, LANES), lambda b, p, i: (b, 0, 0)),
            _const_spec((tk, tk)),
        ],
        out_specs=pl.BlockSpec((1, tq, LANES), lambda b, p, i: (b, i, p)),
        out_shape=jax.ShapeDtypeStruct((B, S, SB_WIDTH), BF16),
        compiler_params=pltpu.CompilerParams(dimension_semantics=("arbitrary",) * 3,
                                             vmem_limit_bytes=VMEM_LIMIT),
        name="sb_prompt",
    )(sbq, kb, vb, _tri(tk))


def _softmax_block(q, kc, m, l, acc, valid=None):
    s = _dot_nt(q, kc)
    if valid is not None:
        s = jnp.where(valid, s, -jnp.inf)
    m_new = jnp.maximum(m, jnp.max(s, axis=-1, keepdims=True))
    alpha = jnp.exp2(m - m_new)
    p = jnp.exp2(s - m_new)
    l = alpha * l + jnp.sum(p, axis=-1, keepdims=True)
    acc = alpha * acc + _dot(p.astype(BF16), kc[:, :MLA_KV_RANK])
    return m_new, l, acc


def _mla_prompt_kernel(q_ref, kc_ref, o_ref, *, t):
    i = pl.program_id(1)
    q = jnp.concatenate([q_ref[0, :, hd * MLA_QW:(hd + 1) * MLA_QW] for hd in range(MLA_HEADS)], axis=0)
    rows = MLA_HEADS * t
    qpos = lax.broadcasted_iota(jnp.int32, (rows, t), 0) & (t - 1)
    kpos = lax.broadcasted_iota(jnp.int32, (rows, t), 1)
    start = pl.multiple_of(i * t, t)
    st = _softmax_block(q, kc_ref[0, pl.ds(start, t), :], jnp.full((rows, 1), -jnp.inf, F32),
                        jnp.zeros((rows, 1), F32), jnp.zeros((rows, MLA_KV_RANK), F32), kpos <= qpos)

    def body(j, st):
        s0 = pl.multiple_of(j * t, t)
        return _softmax_block(q, kc_ref[0, pl.ds(s0, t), :], *st)

    m, l, acc = lax.fori_loop(0, i, body, st)
    out = (acc / l).astype(o_ref.dtype)
    for hd in range(MLA_HEADS):
        o_ref[0, :, hd * MLA_KV_RANK:(hd + 1) * MLA_KV_RANK] = out[hd * t:(hd + 1) * t]


def _mla_prompt(qcat, kcat, *, t):
    B, S, _ = qcat.shape
    return pl.pallas_call(
        functools.partial(_mla_prompt_kernel, t=t),
        grid=(B, S // t),
        in_specs=[
            pl.BlockSpec((1, t, MLA_HEADS * MLA_QW), lambda b, i: (b, i, 0)),
            pl.BlockSpec((1, S, 2 * LANES), lambda b, i: (b, 0, 0)),
        ],
        out_specs=pl.BlockSpec((1, t, MLA_HEADS * MLA_KV_RANK), lambda b, i: (b, i, 0)),
        out_shape=jax.ShapeDtypeStruct((B, S, MLA_HEADS * MLA_KV_RANK), BF16),
        compiler_params=pltpu.CompilerParams(dimension_semantics=("arbitrary",) * 2,
                                             vmem_limit_bytes=VMEM_LIMIT),
        name="mla_prompt",
    )(qcat, kcat)


def _merge_kernel(x_ref, sbo_ref, gsb_ref, mlat_ref, gmla_ref, opool_ref, brg_ref,
                  wsb_ref, wuv_ref, wmla_ref, wpool_ref, wout_ref, npost_ref, y_ref):
    o_sb = (sbo_ref[...].astype(F32) * gsb_ref[...].astype(F32)).astype(BF16)
    merged = brg_ref[:, :D_MODEL].astype(F32) * _dot(o_sb, wsb_ref[...])
    mla_o = _dot(mlat_ref[...], wuv_ref[...])
    o_mla = (mla_o * gmla_ref[...].astype(F32)).astype(BF16)
    merged += brg_ref[:, D_MODEL:2 * D_MODEL].astype(F32) * _dot(o_mla, wmla_ref[...])
    merged += brg_ref[:, 2 * D_MODEL:].astype(F32) * _dot(opool_ref[...], wpool_ref[...])
    o = _dot(merged.astype(BF16), wout_ref[...])
    y_ref[...] = x_ref[...] + _rms(o, npost_ref[...])


def _merge(x2, sbo, gsb, mlat, gmla, opool, brg, lw, *, tm):
    M = x2.shape[0]
    row = lambda w: pl.BlockSpec((tm, w), lambda i: (i, 0))
    return pl.pallas_call(
        _merge_kernel,
        grid=(M // tm,),
        in_specs=[row(D_MODEL), row(SB_WIDTH), row(SB_WIDTH), row(MLA_HEADS * MLA_KV_RANK), row(MLA_WIDTH),
                  row(POOL_WIDTH), row(3 * D_MODEL),
                  _const_spec((SB_WIDTH, D_MODEL)), _const_spec((MLA_HEADS * MLA_KV_RANK, MLA_WIDTH)),
                  _const_spec((MLA_WIDTH, D_MODEL)), _const_spec((POOL_WIDTH, D_MODEL)),
                  _const_spec((D_MODEL, D_MODEL)), _const_spec((1, D_MODEL))],
        out_specs=row(D_MODEL),
        out_shape=jax.ShapeDtypeStruct((M, D_MODEL), F32),
        compiler_params=pltpu.CompilerParams(dimension_semantics=("arbitrary",),
                                             vmem_limit_bytes=VMEM_LIMIT),
        name="merge",
    )(x2, sbo, gsb, mlat, gmla, opool, brg, lw["wsb"], lw["wuvbd"], lw["wmla"], lw["wpool"], lw["wout"],
      lw["npost"])


DEC_BLOCK = 256
DEC_PAGES = 16


def _paged_kernel(pt_ref, q_ref, qc_ref, knew_ref, u_ref, ck_hbm, cv_hbm, cc_hbm, cr_hbm,
                  sbo_ref, mlat_ref,
                  kbuf, vbuf, cbuf, rbuf, sem, carry_ref, sacc_ref, m_ref, l_ref, macc_ref,
                  *, layer, n_chunks):
    b = pl.program_id(0)
    c = pl.program_id(1)
    nb = pl.num_programs(0)
    unit = b * n_chunks + c
    slot = unit % 2
    bufs = ((ck_hbm, kbuf), (cv_hbm, vbuf), (cc_hbm, cbuf), (cr_hbm, rbuf))

    def page_copy(a, page, p, sl):
        hbm, buf = bufs[a]
        return pltpu.make_async_copy(hbm.at[layer, page], buf.at[sl, p], sem.at[a, sl])

    def issue(bb, cc, sl):
        first = (n_chunks - 1 - cc) * DEC_PAGES

        for p in range(DEC_PAGES):
            page = pt_ref[bb, first + p]
            for a in range(4):
                page_copy(a, page, p, sl).start()

    @pl.when(unit == 0)
    def _():
        issue(b, c, slot)

    @pl.when(unit + 1 < nb * n_chunks)
    def _():
        nxt = unit + 1
        issue(nxt // n_chunks, nxt % n_chunks, 1 - slot)

    for p in range(DEC_PAGES):
        for a in range(4):
            page_copy(a, 0, p, slot).wait()

    q_sb = q_ref[0]
    q_lat = qc_ref[0, :, :MLA_KV_RANK]
    q_rope = qc_ref[0, :, MLA_KV_RANK:MLA_KV_RANK + MLA_ROPE_DIM]
    u_tri = u_ref[...]

    @pl.when(c == 0)
    def _():
        carry_ref[...] = jnp.zeros_like(carry_ref)
        sacc_ref[...] = jnp.zeros_like(sacc_ref)
        kn = knew_ref[0].astype(F32)
        s_new = jnp.sum(qc_ref[0].astype(F32) * kn, axis=-1, keepdims=True)
        m_ref[...] = jnp.broadcast_to(s_new, m_ref.shape)
        l_ref[...] = jnp.ones_like(l_ref)
        macc_ref[...] = jnp.broadcast_to(kn[:, :MLA_KV_RANK], macc_ref.shape)

    carry = carry_ref[:, 0:1]
    sacc = sacc_ref[...]
    m = m_ref[:, 0:1]
    l = l_ref[:, 0:1]
    macc = macc_ref[...]
    pages = range(DEC_PAGES)
    lanes = [slice(pg * PAGE_SIZE, (pg + 1) * PAGE_SIZE) for pg in pages]
    z = jnp.concatenate([_dot(q_sb, kbuf[slot, pg].astype(BF16)) for pg in pages], axis=1)
    s = jnp.concatenate([_dot_nt(q_lat, cbuf[slot, pg].astype(BF16)) + _dot(q_rope, rbuf[slot, pg].astype(BF16))
                         for pg in pages], axis=1)

    sp = _softplus2(z)
    blocks = [slice(n * DEC_BLOCK, (n + 1) * DEC_BLOCK) for n in range(DEC_PAGES * PAGE_SIZE // DEC_BLOCK)]
    incl = [_suffix_sums(sp[:, bl], u_tri) for bl in blocks]
    logw = [None] * len(blocks)
    for n in reversed(range(len(blocks))):
        logw[n] = z[:, blocks[n]] - incl[n] - carry
        carry = carry + incl[n][:, 0:1]
    w = jnp.exp2(jnp.concatenate(logw, axis=1)).astype(BF16)
    for pg in pages:
        sacc = sacc + _dot_nt(w[:, lanes[pg]], vbuf[slot, pg].astype(BF16))

    m_new = jnp.maximum(m, jnp.max(s, axis=-1, keepdims=True))
    alpha = jnp.exp2(m - m_new)
    p = jnp.exp2(s - m_new)
    l = alpha * l + jnp.sum(p, axis=-1, keepdims=True)
    p = p.astype(BF16)
    macc = alpha * macc
    for pg in pages:
        macc = macc + _dot(p[:, lanes[pg]], cbuf[slot, pg].astype(BF16))
    m = m_new
    carry_ref[...] = jnp.broadcast_to(carry, carry_ref.shape)
    sacc_ref[...] = sacc
    m_ref[...] = jnp.broadcast_to(m, m_ref.shape)
    l_ref[...] = jnp.broadcast_to(l, l_ref.shape)
    macc_ref[...] = macc

    @pl.when(c == n_chunks - 1)
    def _():
        lane = lax.broadcasted_iota(jnp.int32, (SB_GROUP, LANES), 1)
        sbo_ref[0] = jnp.where(lane < SB_HEAD_DIM, sacc[:SB_GROUP], sacc[SB_GROUP:]).astype(sbo_ref.dtype)
        mlat_ref[0] = (macc / l).astype(mlat_ref.dtype)


def _paged_attention(page_table, q_sb, qcat, knew, ck, cv, cc, cr, *, layer):
    DB, n_pages = page_table.shape
    n_chunks = n_pages // DEC_PAGES
    grid_spec = pltpu.PrefetchScalarGridSpec(
        num_scalar_prefetch=1,
        grid=(DB, n_chunks),
        in_specs=[
            pl.BlockSpec((1, SB_HEADS, SLOT), lambda b, c, pt: (b, 0, 0)),
            pl.BlockSpec((1, MLA_HEADS, MLA_QW), lambda b, c, pt: (b, 0, 0)),
            pl.BlockSpec((1, 1, 2 * LANES), lambda b, c, pt: (b, 0, 0)),
            pl.BlockSpec((DEC_BLOCK, DEC_BLOCK), lambda b, c, pt: (0, 0)),
            pl.BlockSpec(memory_space=pl.ANY),
            pl.BlockSpec(memory_space=pl.ANY),
            pl.BlockSpec(memory_space=pl.ANY),
            pl.BlockSpec(memory_space=pl.ANY),
        ],
        out_specs=[
            pl.BlockSpec((1, SB_GROUP, LANES), lambda b, c, pt: (b, 0, 0)),
            pl.BlockSpec((1, MLA_HEADS, MLA_KV_RANK), lambda b, c, pt: (b, 0, 0)),
        ],
        scratch_shapes=[
            pltpu.VMEM((2, DEC_PAGES, LANES, PAGE_SIZE), F32),
            pltpu.VMEM((2, DEC_PAGES, LANES, PAGE_SIZE), F32),
            pltpu.VMEM((2, DEC_PAGES, PAGE_SIZE, MLA_KV_RANK), F32),
            pltpu.VMEM((2, DEC_PAGES, MLA_ROPE_DIM, PAGE_SIZE), F32),
            pltpu.SemaphoreType.DMA((4, 2)),
            pltpu.VMEM((SB_HEADS, LANES), F32),
            pltpu.VMEM((SB_HEADS, LANES), F32),
            pltpu.VMEM((MLA_HEADS, LANES), F32),
            pltpu.VMEM((MLA_HEADS, LANES), F32),
            pltpu.VMEM((MLA_HEADS, MLA_KV_RANK), F32),
        ],
    )
    return pl.pallas_call(
        functools.partial(_paged_kernel, layer=layer, n_chunks=n_chunks),
        grid_spec=grid_spec,
        out_shape=[jax.ShapeDtypeStruct((DB, SB_GROUP, LANES), BF16),
                   jax.ShapeDtypeStruct((DB, MLA_HEADS, MLA_KV_RANK), BF16)],
        compiler_params=pltpu.CompilerParams(dimension_semantics=("arbitrary", "arbitrary"),
                                             vmem_limit_bytes=VMEM_LIMIT),
        name="paged_attention",
    )(page_table, q_sb, qcat, knew, _tri(DEC_BLOCK), ck, cv, cc, cr)


def _block_diag(blocks):
    n = len(blocks)
    r, c = blocks[0].shape
    rows = []
    for i, blk in enumerate(blocks):
        rows.append(jnp.concatenate([jnp.zeros((r, c * i), blk.dtype), blk,
                                     jnp.zeros((r, c * (n - 1 - i)), blk.dtype)], axis=1))
    return jnp.concatenate(rows, axis=0)


def _rope_cols(w):
    half = MLA_ROPE_DIM // 2
    z = jnp.zeros((w.shape[0], LANES - MLA_ROPE_DIM), w.dtype)
    return jnp.concatenate([w, z], axis=1), jnp.concatenate([-w[:, half:], w[:, :half], z], axis=1)


def _prep_layer(l, norm_pre, norm_post, w_in, mla_q_norm, mla_kv_norm, mla_w_uq, mla_w_uk, mla_w_uv,
                pool_w, pool_scale, w_br_sb, w_br_mla, w_br_pool, w_out):
    w = w_in[l].astype(BF16)
    o = 0
    parts = {}
    for name, width in (("sbq", SB_WIDTH), ("k", LANES), ("v", LANES), ("gsb", SB_WIDTH), ("cq", MLA_Q_RANK),
                        ("ckv", MLA_KV_RANK), ("r", MLA_ROPE_DIM), ("gmla", MLA_WIDTH), ("pu", POOL_WIDTH),
                        ("pg", POOL_WIDTH), ("brg", 3 * D_MODEL)):
        parts[name] = w[:, o:o + width]
        o += width
    z64 = jnp.zeros((D_MODEL, SB_HEAD_DIM), BF16)
    sbq_cols, gsb_cols = [], []
    for s, hd in enumerate(SB_SLOT_HEADS):
        qh = parts["sbq"][:, hd * SB_HEAD_DIM:(hd + 1) * SB_HEAD_DIM]
        sbq_cols += [qh, z64] if s % 2 == 0 else [z64, qh]
        gsb_cols.append(parts["gsb"][:, hd * SB_HEAD_DIM:(hd + 1) * SB_HEAD_DIM])
    r_cols, rs_cols = _rope_cols(parts["r"])
    win = jnp.concatenate([parts["brg"]] + sbq_cols + gsb_cols +
                          [parts["gmla"], parts["pu"], parts["pg"], parts["cq"], parts["k"], parts["v"],
                           parts["ckv"], r_cols, rs_cols], axis=1)

    wuq = mla_w_uq[l].astype(BF16).reshape(MLA_Q_RANK, MLA_HEADS, MLA_NOPE_DIM + MLA_ROPE_DIM)
    wuqn = wuq[:, :, :MLA_NOPE_DIM].reshape(MLA_Q_RANK, MLA_HEADS * MLA_NOPE_DIM)
    rr = [_rope_cols(wuq[:, hd, MLA_NOPE_DIM:]) for hd in range(MLA_HEADS)]
    wuqr = jnp.concatenate([a for a, _ in rr], axis=1)
    wuqrs = jnp.concatenate([b for _, b in rr], axis=1)
    wuk = mla_w_uk[l].astype(BF16)
    wukbd = _block_diag([wuk[:, hd, :].T for hd in range(MLA_HEADS)])
    wuv = mla_w_uv[l].astype(BF16)
    wuvbd = _block_diag([wuv[:, hd, :] for hd in range(MLA_HEADS)])
    wsb = w_br_sb[l].astype(BF16)
    wsb = jnp.concatenate([wsb[hd * SB_HEAD_DIM:(hd + 1) * SB_HEAD_DIM] for hd in SB_SLOT_HEADS], axis=0)
    return dict(
        npre=norm_pre[l].reshape(1, D_MODEL), npost=norm_post[l].reshape(1, D_MODEL), win=win,
        qn=mla_q_norm[l].reshape(1, MLA_Q_RANK), kvn=mla_kv_norm[l].reshape(1, MLA_KV_RANK),
        wuqn=wuqn, wukbd=wukbd, wuqr=wuqr, wuqrs=wuqrs, wuvbd=wuvbd,
        poolw=pool_w[l].astype(BF16), pscale=pool_scale[l].reshape(1, POOL_WIDTH),
        wsb=wsb, wmla=w_br_mla[l].astype(BF16), wpool=w_br_pool[l].astype(BF16), wout=w_out[l].astype(BF16))


def _rope_tables(pos):
    half = MLA_ROPE_DIM // 2
    inv = ROPE_THETA ** (-jnp.arange(half, dtype=F32) / half)
    ang = pos.astype(F32)[:, None] * inv[None, :]
    z = jnp.zeros((pos.shape[0], LANES - MLA_ROPE_DIM), F32)
    cos, sin = jnp.cos(ang), jnp.sin(ang)
    return jnp.concatenate([cos, cos, z], axis=1), jnp.concatenate([sin, sin, z], axis=1)


def _prompt_layer(x, lw, cos, sin):
    B, S, _ = x.shape
    tm = min(ROW_TILE, S)
    a = _inproj(x, lw, cos, sin, tm=tm)
    sbo = _sb_prompt(a["sbq"], a["kb"], a["vb"], tq=min(SB_Q_TILE, S), tk=min(KEY_TILE, S))
    mlat = _mla_prompt(a["qcat"], a["kcat"], t=min(KEY_TILE, S))
    flat = lambda v: v.reshape(B * S, v.shape[-1])
    y = _merge(flat(x), flat(sbo), flat(a["gsb"]), flat(mlat), flat(a["gmla"]), flat(a["opool"]),
               flat(a["brg"]), lw, tm=tm)
    return y.reshape(B, S, D_MODEL), a


def _sample_layer(x, lw, cos, sin, state, page_table, ck, cv, cc, cr, layer, past):
    DB = x.shape[0]
    count = tuple(float(min(past + 1, w)) for w in POOL_WINDOWS)
    a = _inproj(x.reshape(1, DB, D_MODEL), lw, cos, sin, tm=DB, state_t=jnp.transpose(state, (1, 0, 2)),
                decode_count=count)
    q_sb = a["sbq"].reshape(DB, SB_GROUP, 2, SLOT).transpose(0, 2, 1, 3).reshape(DB, SB_HEADS, SLOT)
    sbo, mlat = _paged_attention(page_table, q_sb, a["qcat"].reshape(DB, MLA_HEADS, MLA_QW),
                                 a["kcat"].reshape(DB, 1, 2 * LANES), ck, cv, cc, cr, layer=layer)
    flat = lambda v: v.reshape(DB, v.shape[-1])
    y = _merge(x.reshape(DB, D_MODEL), sbo.reshape(DB, SB_WIDTH), flat(a["gsb"]),
               mlat.reshape(DB, MLA_HEADS * MLA_KV_RANK), flat(a["gmla"]), flat(a["opool"]), flat(a["brg"]),
               lw, tm=DB)
    return y.reshape(DB, 1, D_MODEL), a


def kernel(x_prompt, x_sample, cache_sb_k, cache_sb_v, cache_mla_ckv, cache_mla_krope, state_pool, page_table,
           norm_pre, norm_post, w_in, mla_q_norm, mla_kv_norm, mla_w_uq, mla_w_uk, mla_w_uv,
           pool_w, pool_scale, w_br_sb, w_br_mla, w_br_pool, w_out):
    B, S, _ = x_prompt.shape
    DB, T, _ = x_sample.shape
    assert T == 1
    depth = w_in.shape[0]
    n_pages = page_table.shape[1]
    past = n_pages * PAGE_SIZE
    n_pool = cache_sb_k.shape[1]
    ck = jnp.transpose(cache_sb_k, (0, 1, 3, 4, 2)).reshape(depth, n_pool, LANES, PAGE_SIZE)
    cv = jnp.transpose(cache_sb_v, (0, 1, 3, 4, 2)).reshape(depth, n_pool, LANES, PAGE_SIZE)
    cr = jnp.transpose(cache_mla_krope, (0, 1, 3, 2))

    cos_p, sin_p = _rope_tables(jnp.arange(S, dtype=jnp.int32))
    cos_s, sin_s = _rope_tables(jnp.full((DB,), past, dtype=jnp.int32))

    xp, xs = x_prompt, x_sample
    outs = {n: [] for n in ("kp", "vp", "cp", "rp", "up", "ks", "vs", "cs", "rs", "us")}
    for l in range(depth):
        lw = _prep_layer(l, norm_pre, norm_post, w_in, mla_q_norm, mla_kv_norm, mla_w_uq, mla_w_uk, mla_w_uv,
                         pool_w, pool_scale, w_br_sb, w_br_mla, w_br_pool, w_out)
        xp, a = _prompt_layer(xp, lw, cos_p, sin_p)
        outs["kp"].append(a["k32"].reshape(B, S, SB_KV_HEADS, SB_HEAD_DIM))
        outs["vp"].append(a["v32"].reshape(B, S, SB_KV_HEADS, SB_HEAD_DIM))
        outs["cp"].append(a["ckv32"])
        outs["rp"].append(a["kr32"])
        outs["up"].append(a["pst"][:, 16 - POOL_STATE_LEN:])
        xs, a = _sample_layer(xs, lw, cos_s, sin_s, state_pool[l], page_table, ck, cv, cache_mla_ckv, cr, l, past)
        outs["ks"].append(a["k32"].reshape(DB, 1, SB_KV_HEADS, SB_HEAD_DIM))
        outs["vs"].append(a["v32"].reshape(DB, 1, SB_KV_HEADS, SB_HEAD_DIM))
        outs["cs"].append(a["ckv32"].reshape(DB, 1, MLA_KV_RANK))
        outs["rs"].append(a["kr32"].reshape(DB, 1, MLA_ROPE_DIM))
        outs["us"].append(jnp.concatenate([state_pool[l][:, 1:], a["pst"].reshape(DB, 1, POOL_WIDTH)], axis=1))
    st = lambda n: jnp.stack(outs[n])
    return (xp, xs, st("kp"), st("vp"), st("cp"), st("rp"), st("up"),
            st("ks"), st("vs"), st("cs"), st("rs"), st("us"))
```

```python
import functools
import math

import jax
import jax.numpy as jnp
from jax import lax
from jax.experimental import pallas as pl
from jax.experimental.pallas import tpu as pltpu

F32 = jnp.float32
BF16 = jnp.bfloat16

D_MODEL = 1024
PAGE_SIZE = 128
SB_HEADS = 8
SB_KV_HEADS = 2
SB_HEAD_DIM = 64
SB_GROUP = SB_HEADS // SB_KV_HEADS
SB_WIDTH = SB_HEADS * SB_HEAD_DIM
MLA_HEADS = 8
MLA_Q_RANK = 256
MLA_KV_RANK = 128
MLA_NOPE_DIM = 64
MLA_ROPE_DIM = 32
MLA_V_DIM = 64
MLA_WIDTH = MLA_HEADS * MLA_V_DIM
ROPE_THETA = 10000.0
POOL_WINDOWS = (2, 4, 8, 16)
POOL_WIDTH = 512
POOL_GROUP_DIM = 128
POOL_STATE_LEN = 15
RMS_EPS = 1e-6
LOG2E = math.log2(math.e)

LANES = 128
SLOT = 128
MLA_QW = 256
VMEM_LIMIT = 56 * 1024 * 1024
ROW_TILE = 512
KEY_TILE = 256
SB_Q_TILE = 512

SB_SLOT_HEADS = tuple((s // 2) + SB_GROUP * (s % 2) for s in range(SB_HEADS))

C_BRG = 0
C_SBQ = C_BRG + 3 * D_MODEL
C_GSB = C_SBQ + SB_HEADS * SLOT
C_GMLA = C_GSB + SB_WIDTH
C_PU = C_GMLA + MLA_WIDTH
C_PG = C_PU + POOL_WIDTH
C_CQ = C_PG + POOL_WIDTH
C_K = C_CQ + MLA_Q_RANK
C_V = C_K + LANES
C_CKV = C_V + LANES
C_R = C_CKV + LANES
C_RS = C_R + LANES
IN_PAD = C_RS + LANES


def _dot(a, b):
    return jnp.dot(a, b, preferred_element_type=F32)


def _dot_nt(a, b):
    return lax.dot_general(a, b, (((1,), (1,)), ((), ())), preferred_element_type=F32)


def _sigmoid(x):
    return 1.0 / (1.0 + jnp.exp(-x))


def _rms(x, g):
    return x * lax.rsqrt(jnp.mean(x * x, axis=-1, keepdims=True) + RMS_EPS) * g


def _const_spec(shape):
    nd = len(shape)
    return pl.BlockSpec(shape, lambda *_: (0,) * nd, pipeline_mode=pl.Buffered(1))


def _inproj_kernel(*refs, tm, decode, decode_count):
    (x_ref, npre_ref, win_ref, qn_ref, wuqn_ref, wukbd_ref, wuqr_ref, wuqrs_ref, kvn_ref,
     cos_ref, sin_ref, poolw_ref, pscale_ref) = refs[:13]
    refs = refs[13:]
    if decode:
        state_ref, refs = refs[0], refs[1:]
    (k32_ref, v32_ref, ckv32_ref, kr32_ref, pst_ref, sbq_ref, kb_ref, vb_ref, qcat_ref, kcat_ref,
     gsb_ref, gmla_ref, opool_ref, brg_ref) = refs[:14]
    refs = refs[14:]

    x = x_ref[0]
    h = _rms(x, npre_ref[...]).astype(BF16)

    def proj(off, width):
        return _dot(h, win_ref[:, off:off + width])

    for c in range(0, 3 * D_MODEL, 512):
        brg_ref[0, :, c:c + 512] = _sigmoid(proj(C_BRG + c, 512)).astype(BF16)

    for c in range(0, SB_HEADS * SLOT, 512):
        sbq_ref[0, :, c:c + 512] = (proj(C_SBQ + c, 512) * (LOG2E / math.sqrt(SB_HEAD_DIM))).astype(BF16)
    k = proj(C_K, LANES)
    v = proj(C_V, LANES)
    k32_ref[0] = k
    v32_ref[0] = v
    kb_ref[0] = k.astype(BF16)
    vb_ref[0] = v.astype(BF16)
    g = proj(C_GSB, SB_WIDTH)
    gsb_ref[0] = (g * _sigmoid(g)).astype(BF16)
    g = proj(C_GMLA, MLA_WIDTH)
    gmla_ref[0] = (g * _sigmoid(g)).astype(BF16)

    cos = cos_ref[...]
    sin = sin_ref[...]
    ckv = _rms(proj(C_CKV, LANES), kvn_ref[...])
    krot = proj(C_R, LANES) * cos + proj(C_RS, LANES) * sin
    ckv32_ref[0] = ckv
    kr32_ref[0] = krot[:, :MLA_ROPE_DIM]
    kcat_ref[0, :, :LANES] = ckv.astype(BF16)
    kcat_ref[0, :, LANES:] = krot.astype(BF16)

    cqn = _rms(proj(C_CQ, MLA_Q_RANK), qn_ref[...]).astype(BF16)
    qnope = _dot(cqn, wuqn_ref[...]).astype(BF16)
    qscale = LOG2E / math.sqrt(MLA_NOPE_DIM + MLA_ROPE_DIM)
    for hd in range(MLA_HEADS):
        lo = hd * LANES
        qlat = _dot(qnope, wukbd_ref[:, lo:lo + LANES])
        qrot = _dot(cqn, wuqr_ref[:, lo:lo + LANES]) * cos + _dot(cqn, wuqrs_ref[:, lo:lo + LANES]) * sin
        qcat_ref[0, :, hd * MLA_QW:hd * MLA_QW + LANES] = (qlat * qscale).astype(BF16)
        qcat_ref[0, :, hd * MLA_QW + LANES:(hd + 1) * MLA_QW] = (qrot * qscale).astype(BF16)

    u = proj(C_PU, POOL_WIDTH)
    sums = []
    if decode:
        pst_ref[0] = u
        run = u
        nxt = POOL_STATE_LEN - 1
        for gi, w in enumerate(POOL_WINDOWS):
            while nxt >= POOL_STATE_LEN - (w - 1):
                run = run + state_ref[nxt]
                nxt -= 1
            sums.append(run[:, gi * LANES:(gi + 1) * LANES] * (1.0 / decode_count[gi]))
    else:
        prev_ref = refs[0]
        i = pl.program_id(1)

        @pl.when(i == 0)
        def _():
            prev_ref[...] = jnp.zeros_like(prev_ref)

        ext = jnp.concatenate([prev_ref[...], u], axis=0)
        prev_ref[...] = u[tm - 16:, :]
        pst_ref[0] = u[tm - 16:, :]
        pos = i * tm + lax.broadcasted_iota(jnp.int32, (tm, 1), 0)
        run = ext
        span = 1
        for gi, w in enumerate(POOL_WINDOWS):
            while span < w:
                run = run + pltpu.roll(run, span, 0)
                span *= 2
            cnt = jnp.minimum(pos + 1, w).astype(F32)
            sums.append(run[16:, gi * LANES:(gi + 1) * LANES] / cnt)
    pg = proj(C_PG, POOL_WIDTH)
    pgate = pg * _sigmoid(pg)
    for gi in range(len(POOL_WINDOWS)):
        sl = slice(gi * LANES, (gi + 1) * LANES)
        d = (sums[gi] - u[:, sl]).astype(BF16)
        y = _dot(d, poolw_ref[gi]) * pscale_ref[:, sl]
        opool_ref[0, :, sl] = (y * pgate[:, sl]).astype(BF16)


def _inproj(x, lw, cos, sin, *, tm, state_t=None, decode_count=None):
    B, S, _ = x.shape
    decode = state_t is not None
    nt = S // tm
    row = lambda w: pl.BlockSpec((1, tm, w), lambda b, i: (b, i, 0))
    in_specs = [
        row(D_MODEL),
        _const_spec((1, D_MODEL)),
        _const_spec((D_MODEL, IN_PAD)),
        _const_spec((1, MLA_Q_RANK)),
        _const_spec((MLA_Q_RANK, MLA_HEADS * MLA_NOPE_DIM)),
        _const_spec((MLA_HEADS * MLA_NOPE_DIM, MLA_HEADS * LANES)),
        _const_spec((MLA_Q_RANK, MLA_HEADS * LANES)),
        _const_spec((MLA_Q_RANK, MLA_HEADS * LANES)),
        _const_spec((1, MLA_KV_RANK)),
        pl.BlockSpec((tm, LANES), lambda b, i: (i, 0)),
        pl.BlockSpec((tm, LANES), lambda b, i: (i, 0)),
        _const_spec((len(POOL_WINDOWS), POOL_GROUP_DIM, POOL_GROUP_DIM)),
        _const_spec((1, POOL_WIDTH)),
    ]
    args = [x, lw["npre"], lw["win"], lw["qn"], lw["wuqn"], lw["wukbd"], lw["wuqr"], lw["wuqrs"], lw["kvn"],
            cos, sin, lw["poolw"], lw["pscale"]]
    if decode:
        in_specs.append(_const_spec(state_t.shape))
        args.append(state_t)
        pst_shape, pst_spec = (B, S, POOL_WIDTH), row(POOL_WIDTH)
        scratch = []
    else:
        pst_shape = (B, 16, POOL_WIDTH)
        pst_spec = pl.BlockSpec((1, 16, POOL_WIDTH), lambda b, i: (b, 0, 0))
        scratch = [pltpu.VMEM((16, POOL_WIDTH), F32)]
    outs = [("k32", LANES, F32), ("v32", LANES, F32), ("ckv32", LANES, F32), ("kr32", MLA_ROPE_DIM, F32),
            ("pst", None, F32), ("sbq", SB_HEADS * SLOT, BF16), ("kb", LANES, BF16), ("vb", LANES, BF16),
            ("qcat", MLA_HEADS * MLA_QW, BF16), ("kcat", 2 * LANES, BF16), ("gsb", SB_WIDTH, BF16),
            ("gmla", MLA_WIDTH, BF16), ("opool", POOL_WIDTH, BF16), ("brg", 3 * D_MODEL, BF16)]
    out_shape, out_specs = [], []
    for name, w, dt in outs:
        if name == "pst":
            out_shape.append(jax.ShapeDtypeStruct(pst_shape, dt))
            out_specs.append(pst_spec)
        else:
            out_shape.append(jax.ShapeDtypeStruct((B, S, w), dt))
            out_specs.append(row(w))
    res = pl.pallas_call(
        functools.partial(_inproj_kernel, tm=tm, decode=decode, decode_count=decode_count),
        grid=(B, nt),
        in_specs=in_specs,
        out_specs=out_specs,
        out_shape=out_shape,
        scratch_shapes=scratch,
        compiler_params=pltpu.CompilerParams(dimension_semantics=("arbitrary", "arbitrary"),
                                             vmem_limit_bytes=VMEM_LIMIT),
        name="inproj_decode" if decode else "inproj_prompt",
    )(*args)
    return dict(zip([o[0] for o in outs], res))


def _softplus2(z):
    return jnp.maximum(z, 0.0) + jnp.log(1.0 + jnp.exp2(-jnp.abs(z))) * LOG2E


def _suffix_sums(x, u_tri):
    return _dot(x.astype(BF16), u_tri)


def _sb_weights(z, u_tri, carry, valid=None):
    sp = _softplus2(z)
    if valid is not None:
        sp = jnp.where(valid, sp, 0.0)
    incl = _suffix_sums(sp, u_tri)
    w = jnp.exp2(z - incl - carry)
    if valid is not None:
        w = jnp.where(valid, w, 0.0)
    return w.astype(BF16), carry + incl[:, 0:1]


def _sb_block(q, kblk, vblk, u_tri, carry, valid=None):
    w, carry = _sb_weights(_dot_nt(q, kblk), u_tri, carry, valid)
    return _dot(w, vblk), carry


def _sb_prompt_kernel(q_ref, k_ref, v_ref, u_ref, o_ref, *, tq, tk):
    i = pl.program_id(2)
    u_tri = u_ref[...]
    nsub = tq // tk
    q = jnp.concatenate([q_ref[0, :, :SLOT], q_ref[0, :, SLOT:]], axis=0)
    rows = 2 * tq
    qpos = lax.broadcasted_iota(jnp.int32, (rows, tk), 0) & (tq - 1)
    kpos = lax.broadcasted_iota(jnp.int32, (rows, tk), 1)

    def block(start, carry, valid=None):
        start = pl.multiple_of(start, tk)
        return _sb_block(q, k_ref[0, pl.ds(start, tk), :], v_ref[0, pl.ds(start, tk), :], u_tri, carry, valid)

    acc = jnp.zeros((rows, LANES), F32)
    carry = jnp.zeros((rows, 1), F32)
    for sub in reversed(range(nsub)):
        a, carry = block(i * tq + sub * tk, carry, kpos + sub * tk < qpos)
        acc = acc + a

    def body(jj, st):
        acc, carry = st
        for sub in range(nsub):
            a, carry = block((i - jj) * tq - (sub + 1) * tk, carry)
            acc = acc + a
        return acc, carry

    acc, _ = lax.fori_loop(0, i, body, (acc, carry))
    lane = lax.broadcasted_iota(jnp.int32, (tq, LANES), 1)
    o_ref[0] = jnp.where(lane < SB_HEAD_DIM, acc[:tq], acc[tq:]).astype(o_ref.dtype)


def _tri(t):
    j = lax.broadcasted_iota(jnp.int32, (t, t), 0)
    k = lax.broadcasted_iota(jnp.int32, (t, t), 1)
    return (j >= k).astype(BF16)


def _sb_prompt(sbq, kb, vb, *, tq, tk):
    B, S, _ = sbq.shape
    return pl.pallas_call(
        functools.partial(_sb_prompt_kernel, tq=tq, tk=tk),
        grid=(B, SB_GROUP, S // tq),
        in_specs=[
            pl.BlockSpec((1, tq, 2 * SLOT), lambda b, p, i: (b, i, p)),
            pl.BlockSpec((1, S, LANES), lambda b, p, i: (b, 0, 0)),
            pl.BlockSpec((1, S, LANES), lambda b, p, i: (b, 0, 0)),
            _const_spec((tk, tk)),
        ],
        out_specs=pl.BlockSpec((1, tq, LANES), lambda b, p, i: (b, i, p)),
        out_shape=jax.ShapeDtypeStruct((B, S, SB_WIDTH), BF16),
        compiler_params=pltpu.CompilerParams(dimension_semantics=("arbitrary",) * 3,
                                             vmem_limit_bytes=VMEM_LIMIT),
        name="sb_prompt",
    )(sbq, kb, vb, _tri(tk))


def _softmax_block(q, kc, m, l, acc, valid=None):
    s = _dot_nt(q, kc)
    if valid is not None:
        s = jnp.where(valid, s, -jnp.inf)
    m_new = jnp.maximum(m, jnp.max(s, axis=-1, keepdims=True))
    alpha = jnp.exp2(m - m_new)
    p = jnp.exp2(s - m_new)
    l = alpha * l + jnp.sum(p, axis=-1, keepdims=True)
    acc = alpha * acc + _dot(p.astype(BF16), kc[:, :MLA_KV_RANK])
    return m_new, l, acc


def _mla_prompt_kernel(q_ref, kc_ref, o_ref, *, t):
    i = pl.program_id(1)
    q = jnp.concatenate([q_ref[0, :, hd * MLA_QW:(hd + 1) * MLA_QW] for hd in range(MLA_HEADS)], axis=0)
    rows = MLA_HEADS * t
    qpos = lax.broadcasted_iota(jnp.int32, (rows, t), 0) & (t - 1)
    kpos = lax.broadcasted_iota(jnp.int32, (rows, t), 1)
    start = pl.multiple_of(i * t, t)
    st = _softmax_block(q, kc_ref[0, pl.ds(start, t), :], jnp.full((rows, 1), -jnp.inf, F32),
                        jnp.zeros((rows, 1), F32), jnp.zeros((rows, MLA_KV_RANK), F32), kpos <= qpos)

    def body(j, st):
        s0 = pl.multiple_of(j * t, t)
        return _softmax_block(q, kc_ref[0, pl.ds(s0, t), :], *st)

    m, l, acc = lax.fori_loop(0, i, body, st)
    out = (acc / l).astype(o_ref.dtype)
    for hd in range(MLA_HEADS):
        o_ref[0, :, hd * MLA_KV_RANK:(hd + 1) * MLA_KV_RANK] = out[hd * t:(hd + 1) * t]


def _mla_prompt(qcat, kcat, *, t):
    B, S, _ = qcat.shape
    return pl.pallas_call(
        functools.partial(_mla_prompt_kernel, t=t),
        grid=(B, S // t),
        in_specs=[
            pl.BlockSpec((1, t, MLA_HEADS * MLA_QW), lambda b, i: (b, i, 0)),
            pl.BlockSpec((1, S, 2 * LANES), lambda b, i: (b, 0, 0)),
        ],
        out_specs=pl.BlockSpec((1, t, MLA_HEADS * MLA_KV_RANK), lambda b, i: (b, i, 0)),
        out_shape=jax.ShapeDtypeStruct((B, S, MLA_HEADS * MLA_KV_RANK), BF16),
        compiler_params=pltpu.CompilerParams(dimension_semantics=("arbitrary",) * 2,
                                             vmem_limit_bytes=VMEM_LIMIT),
        name="mla_prompt",
    )(qcat, kcat)


def _merge_kernel(x_ref, sbo_ref, gsb_ref, mlat_ref, gmla_ref, opool_ref, brg_ref,
                  wsb_ref, wuv_ref, wmla_ref, wpool_ref, wout_ref, npost_ref, y_ref):
    o_sb = (sbo_ref[...].astype(F32) * gsb_ref[...].astype(F32)).astype(BF16)
    merged = brg_ref[:, :D_MODEL].astype(F32) * _dot(o_sb, wsb_ref[...])
    mla_o = _dot(mlat_ref[...], wuv_ref[...])
    o_mla = (mla_o * gmla_ref[...].astype(F32)).astype(BF16)
    merged += brg_ref[:, D_MODEL:2 * D_MODEL].astype(F32) * _dot(o_mla, wmla_ref[...])
    merged += brg_ref[:, 2 * D_MODEL:].astype(F32) * _dot(opool_ref[...], wpool_ref[...])
    o = _dot(merged.astype(BF16), wout_ref[...])
    y_ref[...] = x_ref[...] + _rms(o, npost_ref[...])


def _merge(x2, sbo, gsb, mlat, gmla, opool, brg, lw, *, tm):
    M = x2.shape[0]
    row = lambda w: pl.BlockSpec((tm, w), lambda i: (i, 0))
    return pl.pallas_call(
        _merge_kernel,
        grid=(M // tm,),
        in_specs=[row(D_MODEL), row(SB_WIDTH), row(SB_WIDTH), row(MLA_HEADS * MLA_KV_RANK), row(MLA_WIDTH),
                  row(POOL_WIDTH), row(3 * D_MODEL),
                  _const_spec((SB_WIDTH, D_MODEL)), _const_spec((MLA_HEADS * MLA_KV_RANK, MLA_WIDTH)),
                  _const_spec((MLA_WIDTH, D_MODEL)), _const_spec((POOL_WIDTH, D_MODEL)),
                  _const_spec((D_MODEL, D_MODEL)), _const_spec((1, D_MODEL))],
        out_specs=row(D_MODEL),
        out_shape=jax.ShapeDtypeStruct((M, D_MODEL), F32),
        compiler_params=pltpu.CompilerParams(dimension_semantics=("arbitrary",),
                                             vmem_limit_bytes=VMEM_LIMIT),
        name="merge",
    )(x2, sbo, gsb, mlat, gmla, opool, brg, lw["wsb"], lw["wuvbd"], lw["wmla"], lw["wpool"], lw["wout"],
      lw["npost"])


DEC_BLOCK = 256
DEC_PAGES = 16
PAGES_PER_BLOCK = DEC_BLOCK // PAGE_SIZE
DEC_BLOCKS = DEC_PAGES // PAGES_PER_BLOCK


def _paged_chunk(q_sb, q_lat, q_rope, u_tri, kb, vb, cb, rb, state):
    carry, sacc, m, l, macc = state
    blocks = range(DEC_BLOCKS)
    pair = lambda ref, n: ref[PAGES_PER_BLOCK * n:PAGES_PER_BLOCK * (n + 1)].reshape(DEC_BLOCK, LANES).astype(BF16)
    cpairs = [pair(cb, n) for n in blocks]
    z = jnp.concatenate([_dot(q_sb, kb[n].astype(BF16)) for n in blocks], axis=1)
    s = jnp.concatenate([_dot_nt(q_lat, cpairs[n]) + _dot(q_rope, rb[n].astype(BF16)) for n in blocks], axis=1)

    sp = _softplus2(z)
    lanes = [slice(n * DEC_BLOCK, (n + 1) * DEC_BLOCK) for n in blocks]
    incl = _suffix_sums(jnp.concatenate([sp[:, ln] for ln in lanes], axis=0), u_tri)
    for n in reversed(blocks):
        inc = incl[n * SB_HEADS:(n + 1) * SB_HEADS]
        w = jnp.exp2(z[:, lanes[n]] - inc - carry)
        carry = carry + inc[:, 0:1]
        w2 = jnp.concatenate([w[:, :PAGE_SIZE], w[:, PAGE_SIZE:]], axis=0).astype(BF16)
        r = _dot_nt(w2, pair(vb, n))
        sacc = sacc + r[:SB_HEADS, :LANES] + r[SB_HEADS:, LANES:]

    m_new = jnp.maximum(m, jnp.max(s, axis=-1, keepdims=True))
    alpha = jnp.exp2(m - m_new)
    p = jnp.exp2(s - m_new)
    l = alpha * l + jnp.sum(p, axis=-1, keepdims=True)
    p = p.astype(BF16)
    macc = alpha * macc
    for n in blocks:
        macc = macc + _dot(p[:, lanes[n]], cpairs[n])
    return carry, sacc, m_new, l, macc


def _paged_kernel(pt_ref, q_ref, qc_ref, knew_ref, u_ref, ck_hbm, cv_hbm, cc_hbm, cr_hbm,
                  sbo_ref, mlat_ref,
                  k0, v0, c0, r0, k1, v1, c1, r1, sem, carry_ref, sacc_ref, m_ref, l_ref, macc_ref,
                  *, layer, n_chunks):
    b = pl.program_id(0)
    t = pl.program_id(1)
    steps = n_chunks // 2
    step = b * steps + t
    last = pl.num_programs(0) * steps - 1
    slots = ((k0, v0, c0, r0), (k1, v1, c1, r1))
    hbms = (ck_hbm, cv_hbm, cc_hbm, cr_hbm)

    def page_copies(page, p, sl):
        kb, vb, cb, rb = slots[sl]
        half = pl.ds((p % PAGES_PER_BLOCK) * PAGE_SIZE, PAGE_SIZE)
        dsts = (kb.at[p // PAGES_PER_BLOCK, :, half], vb.at[p], cb.at[p], rb.at[p // PAGES_PER_BLOCK, :, half])
        return [pltpu.make_async_copy(hbm.at[layer, page], dst, sem.at[a, sl])
                for a, (hbm, dst) in enumerate(zip(hbms, dsts))]

    def issue(bb, tt, sl):
        first = (n_chunks - 1 - (2 * tt + sl)) * DEC_PAGES
        for p in range(DEC_PAGES):
            for cp in page_copies(pt_ref[bb, first + p], p, sl):
                cp.start()

    def wait(sl):
        for p in range(DEC_PAGES):
            for cp in page_copies(0, p, sl):
                cp.wait()

    @pl.when(step == 0)
    def _():
        issue(b, t, 0)

    q_sb = q_ref[0]
    q_lat = qc_ref[0, :, :MLA_KV_RANK]
    q_rope = qc_ref[0, :, MLA_KV_RANK:MLA_KV_RANK + MLA_ROPE_DIM]
    u_tri = u_ref[...]

    @pl.when(t == 0)
    def _():
        carry_ref[...] = jnp.zeros_like(carry_ref)
        sacc_ref[...] = jnp.zeros_like(sacc_ref)
        kn = knew_ref[0].astype(F32)
        s_new = jnp.sum(qc_ref[0].astype(F32) * kn, axis=-1, keepdims=True)
        m_ref[...] = jnp.broadcast_to(s_new, m_ref.shape)
        l_ref[...] = jnp.ones_like(l_ref)
        macc_ref[...] = jnp.broadcast_to(kn[:, :MLA_KV_RANK], macc_ref.shape)

    state = (carry_ref[:, 0:1], sacc_ref[...], m_ref[:, 0:1], l_ref[:, 0:1], macc_ref[...])
    wait(0)
    issue(b, t, 1)
    state = _paged_chunk(q_sb, q_lat, q_rope, u_tri, *slots[0], state)
    wait(1)
    nxt = jnp.minimum(step + 1, last)
    issue(nxt // steps, nxt % steps, 0)
    carry, sacc, m, l, macc = _paged_chunk(q_sb, q_lat, q_rope, u_tri, *slots[1], state)
    carry_ref[...] = jnp.broadcast_to(carry, carry_ref.shape)
    sacc_ref[...] = sacc
    m_ref[...] = jnp.broadcast_to(m, m_ref.shape)
    l_ref[...] = jnp.broadcast_to(l, l_ref.shape)
    macc_ref[...] = macc

    @pl.when(t == steps - 1)
    def _():
        lane = lax.broadcasted_iota(jnp.int32, (SB_GROUP, LANES), 1)
        sbo_ref[0] = jnp.where(lane < SB_HEAD_DIM, sacc[:SB_GROUP], sacc[SB_GROUP:]).astype(sbo_ref.dtype)
        mlat_ref[0] = (macc / l).astype(mlat_ref.dtype)

    @pl.when(step == last)
    def _():
        wait(0)


def _paged_attention(page_table, q_sb, qcat, knew, ck, cv, cc, cr, *, layer):
    DB, n_pages = page_table.shape
    n_chunks = n_pages // DEC_PAGES
    assert n_chunks % 2 == 0 and n_chunks * DEC_PAGES == n_pages
    slot_bufs = [
        pltpu.VMEM((DEC_BLOCKS, LANES, DEC_BLOCK), F32),
        pltpu.VMEM((DEC_PAGES, LANES, PAGE_SIZE), F32),
        pltpu.VMEM((DEC_PAGES, PAGE_SIZE, MLA_KV_RANK), F32),
        pltpu.VMEM((DEC_BLOCKS, MLA_ROPE_DIM, DEC_BLOCK), F32),
    ]
    grid_spec = pltpu.PrefetchScalarGridSpec(
        num_scalar_prefetch=1,
        grid=(DB, n_chunks // 2),
        in_specs=[
            pl.BlockSpec((1, SB_HEADS, SLOT), lambda b, t, pt: (b, 0, 0)),
            pl.BlockSpec((1, MLA_HEADS, MLA_QW), lambda b, t, pt: (b, 0, 0)),
            pl.BlockSpec((1, 1, 2 * LANES), lambda b, t, pt: (b, 0, 0)),
            pl.BlockSpec((DEC_BLOCK, DEC_BLOCK), lambda b, t, pt: (0, 0)),
            pl.BlockSpec(memory_space=pl.ANY),
            pl.BlockSpec(memory_space=pl.ANY),
            pl.BlockSpec(memory_space=pl.ANY),
            pl.BlockSpec(memory_space=pl.ANY),
        ],
        out_specs=[
            pl.BlockSpec((1, SB_GROUP, LANES), lambda b, t, pt: (b, 0, 0)),
            pl.BlockSpec((1, MLA_HEADS, MLA_KV_RANK), lambda b, t, pt: (b, 0, 0)),
        ],
        scratch_shapes=slot_bufs + slot_bufs + [
            pltpu.SemaphoreType.DMA((4, 2)),
            pltpu.VMEM((SB_HEADS, LANES), F32),
            pltpu.VMEM((SB_HEADS, LANES), F32),
            pltpu.VMEM((MLA_HEADS, LANES), F32),
            pltpu.VMEM((MLA_HEADS, LANES), F32),
            pltpu.VMEM((MLA_HEADS, MLA_KV_RANK), F32),
        ],
    )
    return pl.pallas_call(
        functools.partial(_paged_kernel, layer=layer, n_chunks=n_chunks),
        grid_spec=grid_spec,
        out_shape=[jax.ShapeDtypeStruct((DB, SB_GROUP, LANES), BF16),
                   jax.ShapeDtypeStruct((DB, MLA_HEADS, MLA_KV_RANK), BF16)],
        compiler_params=pltpu.CompilerParams(dimension_semantics=("arbitrary", "arbitrary"),
                                             vmem_limit_bytes=VMEM_LIMIT),
        name="paged_attention",
    )(page_table, q_sb, qcat, knew, _tri(DEC_BLOCK), ck, cv, cc, cr)


def _block_diag(blocks):
    n = len(blocks)
    r, c = blocks[0].shape
    rows = []
    for i, blk in enumerate(blocks):
        rows.append(jnp.concatenate([jnp.zeros((r, c * i), blk.dtype), blk,
                                     jnp.zeros((r, c * (n - 1 - i)), blk.dtype)], axis=1))
    return jnp.concatenate(rows, axis=0)


def _rope_cols(w):
    half = MLA_ROPE_DIM // 2
    z = jnp.zeros((w.shape[0], LANES - MLA_ROPE_DIM), w.dtype)
    return jnp.concatenate([w, z], axis=1), jnp.concatenate([-w[:, half:], w[:, :half], z], axis=1)


def _prep_layer(l, norm_pre, norm_post, w_in, mla_q_norm, mla_kv_norm, mla_w_uq, mla_w_uk, mla_w_uv,
                pool_w, pool_scale, w_br_sb, w_br_mla, w_br_pool, w_out):
    w = w_in[l].astype(BF16)
    o = 0
    parts = {}
    for name, width in (("sbq", SB_WIDTH), ("k", LANES), ("v", LANES), ("gsb", SB_WIDTH), ("cq", MLA_Q_RANK),
                        ("ckv", MLA_KV_RANK), ("r", MLA_ROPE_DIM), ("gmla", MLA_WIDTH), ("pu", POOL_WIDTH),
                        ("pg", POOL_WIDTH), ("brg", 3 * D_MODEL)):
        parts[name] = w[:, o:o + width]
        o += width
    z64 = jnp.zeros((D_MODEL, SB_HEAD_DIM), BF16)
    sbq_cols, gsb_cols = [], []
    for s, hd in enumerate(SB_SLOT_HEADS):
        qh = parts["sbq"][:, hd * SB_HEAD_DIM:(hd + 1) * SB_HEAD_DIM]
        sbq_cols += [qh, z64] if s % 2 == 0 else [z64, qh]
        gsb_cols.append(parts["gsb"][:, hd * SB_HEAD_DIM:(hd + 1) * SB_HEAD_DIM])
    r_cols, rs_cols = _rope_cols(parts["r"])
    win = jnp.concatenate([parts["brg"]] + sbq_cols + gsb_cols +
                          [parts["gmla"], parts["pu"], parts["pg"], parts["cq"], parts["k"], parts["v"],
                           parts["ckv"], r_cols, rs_cols], axis=1)

    wuq = mla_w_uq[l].astype(BF16).reshape(MLA_Q_RANK, MLA_HEADS, MLA_NOPE_DIM + MLA_ROPE_DIM)
    wuqn = wuq[:, :, :MLA_NOPE_DIM].reshape(MLA_Q_RANK, MLA_HEADS * MLA_NOPE_DIM)
    rr = [_rope_cols(wuq[:, hd, MLA_NOPE_DIM:]) for hd in range(MLA_HEADS)]
    wuqr = jnp.concatenate([a for a, _ in rr], axis=1)
    wuqrs = jnp.concatenate([b for _, b in rr], axis=1)
    wuk = mla_w_uk[l].astype(BF16)
    wukbd = _block_diag([wuk[:, hd, :].T for hd in range(MLA_HEADS)])
    wuv = mla_w_uv[l].astype(BF16)
    wuvbd = _block_diag([wuv[:, hd, :] for hd in range(MLA_HEADS)])
    wsb = w_br_sb[l].astype(BF16)
    wsb = jnp.concatenate([wsb[hd * SB_HEAD_DIM:(hd + 1) * SB_HEAD_DIM] for hd in SB_SLOT_HEADS], axis=0)
    return dict(
        npre=norm_pre[l].reshape(1, D_MODEL), npost=norm_post[l].reshape(1, D_MODEL), win=win,
        qn=mla_q_norm[l].reshape(1, MLA_Q_RANK), kvn=mla_kv_norm[l].reshape(1, MLA_KV_RANK),
        wuqn=wuqn, wukbd=wukbd, wuqr=wuqr, wuqrs=wuqrs, wuvbd=wuvbd,
        poolw=pool_w[l].astype(BF16), pscale=pool_scale[l].reshape(1, POOL_WIDTH),
        wsb=wsb, wmla=w_br_mla[l].astype(BF16), wpool=w_br_pool[l].astype(BF16), wout=w_out[l].astype(BF16))


def _rope_tables(pos):
    half = MLA_ROPE_DIM // 2
    inv = ROPE_THETA ** (-jnp.arange(half, dtype=F32) / half)
    ang = pos.astype(F32)[:, None] * inv[None, :]
    z = jnp.zeros((pos.shape[0], LANES - MLA_ROPE_DIM), F32)
    cos, sin = jnp.cos(ang), jnp.sin(ang)
    return jnp.concatenate([cos, cos, z], axis=1), jnp.concatenate([sin, sin, z], axis=1)


def _prompt_layer(x, lw, cos, sin):
    B, S, _ = x.shape
    tm = min(ROW_TILE, S)
    a = _inproj(x, lw, cos, sin, tm=tm)
    sbo = _sb_prompt(a["sbq"], a["kb"], a["vb"], tq=min(SB_Q_TILE, S), tk=min(KEY_TILE, S))
    mlat = _mla_prompt(a["qcat"], a["kcat"], t=min(KEY_TILE, S))
    flat = lambda v: v.reshape(B * S, v.shape[-1])
    y = _merge(flat(x), flat(sbo), flat(a["gsb"]), flat(mlat), flat(a["gmla"]), flat(a["opool"]),
               flat(a["brg"]), lw, tm=tm)
    return y.reshape(B, S, D_MODEL), a


def _sample_layer(x, lw, cos, sin, state, page_table, ck, cv, cc, cr, layer, past):
    DB = x.shape[0]
    count = tuple(float(min(past + 1, w)) for w in POOL_WINDOWS)
    a = _inproj(x.reshape(1, DB, D_MODEL), lw, cos, sin, tm=DB, state_t=jnp.transpose(state, (1, 0, 2)),
                decode_count=count)
    q_sb = a["sbq"].reshape(DB, SB_GROUP, 2, SLOT).transpose(0, 2, 1, 3).reshape(DB, SB_HEADS, SLOT)
    sbo, mlat = _paged_attention(page_table, q_sb, a["qcat"].reshape(DB, MLA_HEADS, MLA_QW),
                                 a["kcat"].reshape(DB, 1, 2 * LANES), ck, cv, cc, cr, layer=layer)
    flat = lambda v: v.reshape(DB, v.shape[-1])
    y = _merge(x.reshape(DB, D_MODEL), sbo.reshape(DB, SB_WIDTH), flat(a["gsb"]),
               mlat.reshape(DB, MLA_HEADS * MLA_KV_RANK), flat(a["gmla"]), flat(a["opool"]), flat(a["brg"]),
               lw, tm=DB)
    return y.reshape(DB, 1, D_MODEL), a


def kernel(x_prompt, x_sample, cache_sb_k, cache_sb_v, cache_mla_ckv, cache_mla_krope, state_pool, page_table,
           norm_pre, norm_post, w_in, mla_q_norm, mla_kv_norm, mla_w_uq, mla_w_uk, mla_w_uv,
           pool_w, pool_scale, w_br_sb, w_br_mla, w_br_pool, w_out):
    B, S, _ = x_prompt.shape
    DB, T, _ = x_sample.shape
    assert T == 1
    depth = w_in.shape[0]
    n_pages = page_table.shape[1]
    past = n_pages * PAGE_SIZE
    n_pool = cache_sb_k.shape[1]
    ck = jnp.transpose(cache_sb_k, (0, 1, 3, 4, 2)).reshape(depth, n_pool, LANES, PAGE_SIZE)
    cv = jnp.transpose(cache_sb_v, (0, 1, 3, 4, 2)).reshape(depth, n_pool, LANES, PAGE_SIZE)
    cr = jnp.transpose(cache_mla_krope, (0, 1, 3, 2))

    cos_p, sin_p = _rope_tables(jnp.arange(S, dtype=jnp.int32))
    cos_s, sin_s = _rope_tables(jnp.full((DB,), past, dtype=jnp.int32))

    xp, xs = x_prompt, x_sample
    outs = {n: [] for n in ("kp", "vp", "cp", "rp", "up", "ks", "vs", "cs", "rs", "us")}
    for l in range(depth):
        lw = _prep_layer(l, norm_pre, norm_post, w_in, mla_q_norm, mla_kv_norm, mla_w_uq, mla_w_uk, mla_w_uv,
                         pool_w, pool_scale, w_br_sb, w_br_mla, w_br_pool, w_out)
        xp, a = _prompt_layer(xp, lw, cos_p, sin_p)
        outs["kp"].append(a["k32"].reshape(B, S, SB_KV_HEADS, SB_HEAD_DIM))
        outs["vp"].append(a["v32"].reshape(B, S, SB_KV_HEADS, SB_HEAD_DIM))
        outs["cp"].append(a["ckv32"])
        outs["rp"].append(a["kr32"])
        outs["up"].append(a["pst"][:, 16 - POOL_STATE_LEN:])
        xs, a = _sample_layer(xs, lw, cos_s, sin_s, state_pool[l], page_table, ck, cv, cache_mla_ckv, cr, l, past)
        outs["ks"].append(a["k32"].reshape(DB, 1, SB_KV_HEADS, SB_HEAD_DIM))
        outs["vs"].append(a["v32"].reshape(DB, 1, SB_KV_HEADS, SB_HEAD_DIM))
        outs["cs"].append(a["ckv32"].reshape(DB, 1, MLA_KV_RANK))
        outs["rs"].append(a["kr32"].reshape(DB, 1, MLA_ROPE_DIM))
        outs["us"].append(jnp.concatenate([state_pool[l][:, 1:], a["pst"].reshape(DB, 1, POOL_WIDTH)], axis=1))
    st = lambda n: jnp.stack(outs[n])
    return (xp, xs, st("kp"), st("vp"), st("cp"), st("rp"), st("up"),
            st("ks"), st("vs"), st("cs"), st("rs"), st("us"))
```

```python
import functools
import math

import jax
import jax.numpy as jnp
from jax import lax
from jax.experimental import pallas as pl
from jax.experimental.pallas import tpu as pltpu

F32 = jnp.float32
BF16 = jnp.bfloat16

D_MODEL = 1024
PAGE_SIZE = 128
SB_HEADS = 8
SB_KV_HEADS = 2
SB_HEAD_DIM = 64
SB_GROUP = SB_HEADS // SB_KV_HEADS
SB_WIDTH = SB_HEADS * SB_HEAD_DIM
MLA_HEADS = 8
MLA_Q_RANK = 256
MLA_KV_RANK = 128
MLA_NOPE_DIM = 64
MLA_ROPE_DIM = 32
MLA_V_DIM = 64
MLA_WIDTH = MLA_HEADS * MLA_V_DIM
ROPE_THETA = 10000.0
POOL_WINDOWS = (2, 4, 8, 16)
POOL_WIDTH = 512
POOL_GROUP_DIM = 128
POOL_STATE_LEN = 15
RMS_EPS = 1e-6
LOG2E = math.log2(math.e)

LANES = 128
SLOT = 128
MLA_QW = 256
VMEM_LIMIT = 56 * 1024 * 1024
ROW_TILE = 512
KEY_TILE = 256
SB_Q_TILE = 512

SB_SLOT_HEADS = tuple((s // 2) + SB_GROUP * (s % 2) for s in range(SB_HEADS))

C_BRG = 0
C_SBQ = C_BRG + 3 * D_MODEL
C_GSB = C_SBQ + SB_HEADS * SLOT
C_GMLA = C_GSB + SB_WIDTH
C_PU = C_GMLA + MLA_WIDTH
C_PG = C_PU + POOL_WIDTH
C_CQ = C_PG + POOL_WIDTH
C_K = C_CQ + MLA_Q_RANK
C_V = C_K + LANES
C_CKV = C_V + LANES
C_R = C_CKV + LANES
C_RS = C_R + LANES
IN_PAD = C_RS + LANES


def _dot(a, b):
    return jnp.dot(a, b, preferred_element_type=F32)


def _dot_nt(a, b):
    return lax.dot_general(a, b, (((1,), (1,)), ((), ())), preferred_element_type=F32)


def _sigmoid(x):
    return 1.0 / (1.0 + jnp.exp(-x))


def _rms(x, g):
    return x * lax.rsqrt(jnp.mean(x * x, axis=-1, keepdims=True) + RMS_EPS) * g


def _const_spec(shape):
    nd = len(shape)
    return pl.BlockSpec(shape, lambda *_: (0,) * nd, pipeline_mode=pl.Buffered(1))


def _inproj_kernel(*refs, tm, decode, decode_count):
    (x_ref, npre_ref, win_ref, qn_ref, wuqn_ref, wukbd_ref, wuqr_ref, wuqrs_ref, kvn_ref,
     cos_ref, sin_ref, poolw_ref, pscale_ref) = refs[:13]
    refs = refs[13:]
    if decode:
        state_ref, refs = refs[0], refs[1:]
    (k32_ref, v32_ref, ckv32_ref, kr32_ref, pst_ref, sbq_ref, kb_ref, vb_ref, qcat_ref, kcat_ref,
     gsb_ref, gmla_ref, opool_ref, brg_ref) = refs[:14]
    refs = refs[14:]

    x = x_ref[0]
    h = _rms(x, npre_ref[...]).astype(BF16)

    def proj(off, width):
        return _dot(h, win_ref[:, off:off + width])

    for c in range(0, 3 * D_MODEL, 512):
        brg_ref[0, :, c:c + 512] = _sigmoid(proj(C_BRG + c, 512)).astype(BF16)

    for c in range(0, SB_HEADS * SLOT, 512):
        sbq_ref[0, :, c:c + 512] = (proj(C_SBQ + c, 512) * (LOG2E / math.sqrt(SB_HEAD_DIM))).astype(BF16)
    k = proj(C_K, LANES)
    v = proj(C_V, LANES)
    k32_ref[0] = k
    v32_ref[0] = v
    kb_ref[0] = k.astype(BF16)
    vb_ref[0] = v.astype(BF16)
    g = proj(C_GSB, SB_WIDTH)
    gsb_ref[0] = (g * _sigmoid(g)).astype(BF16)
    g = proj(C_GMLA, MLA_WIDTH)
    gmla_ref[0] = (g * _sigmoid(g)).astype(BF16)

    cos = cos_ref[...]
    sin = sin_ref[...]
    ckv = _rms(proj(C_CKV, LANES), kvn_ref[...])
    krot = proj(C_R, LANES) * cos + proj(C_RS, LANES) * sin
    ckv32_ref[0] = ckv
    kr32_ref[0] = krot[:, :MLA_ROPE_DIM]
    kcat_ref[0, :, :LANES] = ckv.astype(BF16)
    kcat_ref[0, :, LANES:] = krot.astype(BF16)

    cqn = _rms(proj(C_CQ, MLA_Q_RANK), qn_ref[...]).astype(BF16)
    qnope = _dot(cqn, wuqn_ref[...]).astype(BF16)
    qscale = LOG2E / math.sqrt(MLA_NOPE_DIM + MLA_ROPE_DIM)
    for hd in range(MLA_HEADS):
        lo = hd * LANES
        qlat = _dot(qnope, wukbd_ref[:, lo:lo + LANES])
        qrot = _dot(cqn, wuqr_ref[:, lo:lo + LANES]) * cos + _dot(cqn, wuqrs_ref[:, lo:lo + LANES]) * sin
        qcat_ref[0, :, hd * MLA_QW:hd * MLA_QW + LANES] = (qlat * qscale).astype(BF16)
        qcat_ref[0, :, hd * MLA_QW + LANES:(hd + 1) * MLA_QW] = (qrot * qscale).astype(BF16)

    u = proj(C_PU, POOL_WIDTH)
    sums = []
    if decode:
        pst_ref[0] = u
        run = u
        nxt = POOL_STATE_LEN - 1
        for gi, w in enumerate(POOL_WINDOWS):
            while nxt >= POOL_STATE_LEN - (w - 1):
                run = run + state_ref[nxt]
                nxt -= 1
            sums.append(run[:, gi * LANES:(gi + 1) * LANES] * (1.0 / decode_count[gi]))
    else:
        prev_ref = refs[0]
        i = pl.program_id(1)

        @pl.when(i == 0)
        def _():
            prev_ref[...] = jnp.zeros_like(prev_ref)

        ext = jnp.concatenate([prev_ref[...], u], axis=0)
        prev_ref[...] = u[tm - 16:, :]
        pst_ref[0] = u[tm - 16:, :]
        pos = i * tm + lax.broadcasted_iota(jnp.int32, (tm, 1), 0)
        run = ext
        span = 1
        for gi, w in enumerate(POOL_WINDOWS):
            while span < w:
                run = run + pltpu.roll(run, span, 0)
                span *= 2
            cnt = jnp.minimum(pos + 1, w).astype(F32)
            sums.append(run[16:, gi * LANES:(gi + 1) * LANES] / cnt)
    pg = proj(C_PG, POOL_WIDTH)
    pgate = pg * _sigmoid(pg)
    for gi in range(len(POOL_WINDOWS)):
        sl = slice(gi * LANES, (gi + 1) * LANES)
        d = (sums[gi] - u[:, sl]).astype(BF16)
        y = _dot(d, poolw_ref[gi]) * pscale_ref[:, sl]
        opool_ref[0, :, sl] = (y * pgate[:, sl]).astype(BF16)


def _inproj(x, lw, cos, sin, *, tm, state_t=None, decode_count=None):
    B, S, _ = x.shape
    decode = state_t is not None
    nt = S // tm
    row = lambda w: pl.BlockSpec((1, tm, w), lambda b, i: (b, i, 0))
    in_specs = [
        row(D_MODEL),
        _const_spec((1, D_MODEL)),
        _const_spec((D_MODEL, IN_PAD)),
        _const_spec((1, MLA_Q_RANK)),
        _const_spec((MLA_Q_RANK, MLA_HEADS * MLA_NOPE_DIM)),
        _const_spec((MLA_HEADS * MLA_NOPE_DIM, MLA_HEADS * LANES)),
        _const_spec((MLA_Q_RANK, MLA_HEADS * LANES)),
        _const_spec((MLA_Q_RANK, MLA_HEADS * LANES)),
        _const_spec((1, MLA_KV_RANK)),
        pl.BlockSpec((tm, LANES), lambda b, i: (i, 0)),
        pl.BlockSpec((tm, LANES), lambda b, i: (i, 0)),
        _const_spec((len(POOL_WINDOWS), POOL_GROUP_DIM, POOL_GROUP_DIM)),
        _const_spec((1, POOL_WIDTH)),
    ]
    args = [x, lw["npre"], lw["win"], lw["qn"], lw["wuqn"], lw["wukbd"], lw["wuqr"], lw["wuqrs"], lw["kvn"],
            cos, sin, lw["poolw"], lw["pscale"]]
    if decode:
        in_specs.append(_const_spec(state_t.shape))
        args.append(state_t)
        pst_shape, pst_spec = (B, S, POOL_WIDTH), row(POOL_WIDTH)
        scratch = []
    else:
        pst_shape = (B, 16, POOL_WIDTH)
        pst_spec = pl.BlockSpec((1, 16, POOL_WIDTH), lambda b, i: (b, 0, 0))
        scratch = [pltpu.VMEM((16, POOL_WIDTH), F32)]
    outs = [("k32", LANES, F32), ("v32", LANES, F32), ("ckv32", LANES, F32), ("kr32", MLA_ROPE_DIM, F32),
            ("pst", None, F32), ("sbq", SB_HEADS * SLOT, BF16), ("kb", LANES, BF16), ("vb", LANES, BF16),
            ("qcat", MLA_HEADS * MLA_QW, BF16), ("kcat", 2 * LANES, BF16), ("gsb", SB_WIDTH, BF16),
            ("gmla", MLA_WIDTH, BF16), ("opool", POOL_WIDTH, BF16), ("brg", 3 * D_MODEL, BF16)]
    out_shape, out_specs = [], []
    for name, w, dt in outs:
        if name == "pst":
            out_shape.append(jax.ShapeDtypeStruct(pst_shape, dt))
            out_specs.append(pst_spec)
        else:
            out_shape.append(jax.ShapeDtypeStruct((B, S, w), dt))
            out_specs.append(row(w))
    res = pl.pallas_call(
        functools.partial(_inproj_kernel, tm=tm, decode=decode, decode_count=decode_count),
        grid=(B, nt),
        in_specs=in_specs,
        out_specs=out_specs,
        out_shape=out_shape,
        scratch_shapes=scratch,
        compiler_params=pltpu.CompilerParams(dimension_semantics=("arbitrary", "arbitrary"),
                                             vmem_limit_bytes=VMEM_LIMIT),
        name="inproj_decode" if decode else "inproj_prompt",
    )(*args)
    return dict(zip([o[0] for o in outs], res))


def _softplus2(z):
    return jnp.maximum(z, 0.0) + jnp.log(1.0 + jnp.exp2(-jnp.abs(z))) * LOG2E


def _suffix_sums(x, u_tri):
    return _dot(x.astype(BF16), u_tri)


def _sb_weights(z, u_tri, carry, valid=None):
    sp = _softplus2(z)
    if valid is not None:
        sp = jnp.where(valid, sp, 0.0)
    incl = _suffix_sums(sp, u_tri)
    w = jnp.exp2(z - incl - carry)
    if valid is not None:
        w = jnp.where(valid, w, 0.0)
    return w.astype(BF16), carry + incl[:, 0:1]


def _sb_block(q, kblk, vblk, u_tri, carry, valid=None):
    w, carry = _sb_weights(_dot_nt(q, kblk), u_tri, carry, valid)
    return _dot(w, vblk), carry


def _sb_prompt_body(q_ref, k_ref, v_ref, u_tri, o_ref, i, *, tq, tk):
    nsub = tq // tk
    q = jnp.concatenate([q_ref[0, :, :SLOT], q_ref[0, :, SLOT:]], axis=0)
    rows = 2 * tq
    qpos = lax.broadcasted_iota(jnp.int32, (rows, tk), 0) & (tq - 1)
    kpos = lax.broadcasted_iota(jnp.int32, (rows, tk), 1)

    def block(start, carry, valid=None):
        start = pl.multiple_of(start, tk)
        return _sb_block(q, k_ref[0, pl.ds(start, tk), :], v_ref[0, pl.ds(start, tk), :], u_tri, carry, valid)

    acc = jnp.zeros((rows, LANES), F32)
    carry = jnp.zeros((rows, 1), F32)
    for sub in reversed(range(nsub)):
        a, carry = block(i * tq + sub * tk, carry, kpos + sub * tk < qpos)
        acc = acc + a

    def body(jj, st):
        acc, carry = st
        for sub in range(nsub):
            a, carry = block((i - jj) * tq - (sub + 1) * tk, carry)
            acc = acc + a
        return acc, carry

    acc, _ = lax.fori_loop(0, i, body, (acc, carry))
    lane = lax.broadcasted_iota(jnp.int32, (tq, LANES), 1)
    o_ref[0] = jnp.where(lane < SB_HEAD_DIM, acc[:tq], acc[tq:]).astype(o_ref.dtype)


def _tri(t):
    j = lax.broadcasted_iota(jnp.int32, (t, t), 0)
    k = lax.broadcasted_iota(jnp.int32, (t, t), 1)
    return (j >= k).astype(BF16)


def _softmax_block(q, kc, m, l, acc, valid=None):
    s = _dot_nt(q, kc)
    if valid is not None:
        s = jnp.where(valid, s, -jnp.inf)
    m_new = jnp.maximum(m, jnp.max(s, axis=-1, keepdims=True))
    alpha = jnp.exp2(m - m_new)
    p = jnp.exp2(s - m_new)
    l = alpha * l + jnp.sum(p, axis=-1, keepdims=True)
    acc = alpha * acc + _dot(p.astype(BF16), kc[:, :MLA_KV_RANK])
    return m_new, l, acc


def _mla_prompt_kernel(q_ref, kc_ref, o_ref, *, t):
    i = pl.program_id(1)
    q = jnp.concatenate([q_ref[0, :, hd * MLA_QW:(hd + 1) * MLA_QW] for hd in range(MLA_HEADS)], axis=0)
    rows = MLA_HEADS * t
    qpos = lax.broadcasted_iota(jnp.int32, (rows, t), 0) & (t - 1)
    kpos = lax.broadcasted_iota(jnp.int32, (rows, t), 1)
    start = pl.multiple_of(i * t, t)
    st = _softmax_block(q, kc_ref[0, pl.ds(start, t), :], jnp.full((rows, 1), -jnp.inf, F32),
                        jnp.zeros((rows, 1), F32), jnp.zeros((rows, MLA_KV_RANK), F32), kpos <= qpos)

    def body(j, st):
        s0 = pl.multiple_of(j * t, t)
        return _softmax_block(q, kc_ref[0, pl.ds(s0, t), :], *st)

    m, l, acc = lax.fori_loop(0, i, body, st)
    out = (acc / l).astype(o_ref.dtype)
    for hd in range(MLA_HEADS):
        o_ref[0, :, hd * MLA_KV_RANK:(hd + 1) * MLA_KV_RANK] = out[hd * t:(hd + 1) * t]


def _mla_prompt(qcat, kcat, *, t):
    B, S, _ = qcat.shape
    return pl.pallas_call(
        functools.partial(_mla_prompt_kernel, t=t),
        grid=(B, S // t),
        in_specs=[
            pl.BlockSpec((1, t, MLA_HEADS * MLA_QW), lambda b, i: (b, i, 0)),
            pl.BlockSpec((1, S, 2 * LANES), lambda b, i: (b, 0, 0)),
        ],
        out_specs=pl.BlockSpec((1, t, MLA_HEADS * MLA_KV_RANK), lambda b, i: (b, i, 0)),
        out_shape=jax.ShapeDtypeStruct((B, S, MLA_HEADS * MLA_KV_RANK), BF16),
        compiler_params=pltpu.CompilerParams(dimension_semantics=("arbitrary",) * 2,
                                             vmem_limit_bytes=VMEM_LIMIT),
        name="mla_prompt",
    )(qcat, kcat)


def _merge_kernel(x_ref, sbo_ref, gsb_ref, mlat_ref, gmla_ref, opool_ref, brg_ref,
                  wsb_ref, wuv_ref, wmla_ref, wpool_ref, wout_ref, npost_ref, y_ref):
    o_sb = (sbo_ref[...].astype(F32) * gsb_ref[...].astype(F32)).astype(BF16)
    merged = brg_ref[:, :D_MODEL].astype(F32) * _dot(o_sb, wsb_ref[...])
    mla_o = _dot(mlat_ref[...], wuv_ref[...])
    o_mla = (mla_o * gmla_ref[...].astype(F32)).astype(BF16)
    merged += brg_ref[:, D_MODEL:2 * D_MODEL].astype(F32) * _dot(o_mla, wmla_ref[...])
    merged += brg_ref[:, 2 * D_MODEL:].astype(F32) * _dot(opool_ref[...], wpool_ref[...])
    o = _dot(merged.astype(BF16), wout_ref[...])
    y_ref[...] = x_ref[...] + _rms(o, npost_ref[...])


def _merge(x2, sbo, gsb, mlat, gmla, opool, brg, lw, *, tm):
    M = x2.shape[0]
    row = lambda w: pl.BlockSpec((tm, w), lambda i: (i, 0))
    return pl.pallas_call(
        _merge_kernel,
        grid=(M // tm,),
        in_specs=[row(D_MODEL), row(SB_WIDTH), row(SB_WIDTH), row(MLA_HEADS * MLA_KV_RANK), row(MLA_WIDTH),
                  row(POOL_WIDTH), row(3 * D_MODEL),
                  _const_spec((SB_WIDTH, D_MODEL)), _const_spec((MLA_HEADS * MLA_KV_RANK, MLA_WIDTH)),
                  _const_spec((MLA_WIDTH, D_MODEL)), _const_spec((POOL_WIDTH, D_MODEL)),
                  _const_spec((D_MODEL, D_MODEL)), _const_spec((1, D_MODEL))],
        out_specs=row(D_MODEL),
        out_shape=jax.ShapeDtypeStruct((M, D_MODEL), F32),
        compiler_params=pltpu.CompilerParams(dimension_semantics=("arbitrary",),
                                             vmem_limit_bytes=VMEM_LIMIT),
        name="merge",
    )(x2, sbo, gsb, mlat, gmla, opool, brg, lw["wsb"], lw["wuvbd"], lw["wmla"], lw["wpool"], lw["wout"],
      lw["npost"])


DEC_BLOCK = 256
DEC_PAGES = 16
PAGES_PER_BLOCK = DEC_BLOCK // PAGE_SIZE
DEC_BLOCKS = DEC_PAGES // PAGES_PER_BLOCK


def _paged_chunk(q_sb, q_lat, q_rope, u_tri, kb, vb, cb, rb, state):
    carry, sacc, m, l, macc = state
    blocks = range(DEC_BLOCKS)
    pair = lambda ref, n: ref[PAGES_PER_BLOCK * n:PAGES_PER_BLOCK * (n + 1)].reshape(DEC_BLOCK, LANES).astype(BF16)
    cpairs = [pair(cb, n) for n in blocks]
    z = jnp.concatenate([_dot(q_sb, kb[n].astype(BF16)) for n in blocks], axis=1)
    s = jnp.concatenate([_dot_nt(q_lat, cpairs[n]) + _dot(q_rope, rb[n].astype(BF16)) for n in blocks], axis=1)

    sp = _softplus2(z)
    lanes = [slice(n * DEC_BLOCK, (n + 1) * DEC_BLOCK) for n in blocks]
    incl = _suffix_sums(jnp.concatenate([sp[:, ln] for ln in lanes], axis=0), u_tri)
    for n in reversed(blocks):
        inc = incl[n * SB_HEADS:(n + 1) * SB_HEADS]
        w = jnp.exp2(z[:, lanes[n]] - inc - carry)
        carry = carry + inc[:, 0:1]
        w2 = jnp.concatenate([w[:, :PAGE_SIZE], w[:, PAGE_SIZE:]], axis=0).astype(BF16)
        r = _dot_nt(w2, pair(vb, n))
        sacc = sacc + r[:SB_HEADS, :LANES] + r[SB_HEADS:, LANES:]

    m_new = jnp.maximum(m, jnp.max(s, axis=-1, keepdims=True))
    alpha = jnp.exp2(m - m_new)
    p = jnp.exp2(s - m_new)
    l = alpha * l + jnp.sum(p, axis=-1, keepdims=True)
    p = p.astype(BF16)
    macc = alpha * macc
    for n in blocks:
        macc = macc + _dot(p[:, lanes[n]], cpairs[n])
    return carry, sacc, m_new, l, macc


def _sb_paged_kernel(pt_ref, q_ref, k_ref, v_ref, u_ref, qs_ref, qc_ref, knew_ref, ck_hbm, cv_hbm, cc_hbm, cr_hbm,
                     o_ref, sbo_ref, mlat_ref, kbuf, vbuf, cbuf, rbuf, sem, *, layer, n_pages, tq, tk):
    i = pl.program_id(2)
    step = (pl.program_id(0) * pl.num_programs(1) + pl.program_id(1)) * pl.num_programs(2) + i
    last = pl.num_programs(0) * pl.num_programs(1) * pl.num_programs(2) - 1
    slot = step % 2
    u_tri = u_ref[...]
    hbms = (ck_hbm, cv_hbm, cc_hbm, cr_hbm)

    def page_copies(page, p, sl):
        half = pl.ds((p % PAGES_PER_BLOCK) * PAGE_SIZE, PAGE_SIZE)
        dsts = (kbuf.at[sl, p // PAGES_PER_BLOCK, :, half], vbuf.at[sl, p], cbuf.at[sl, p],
                rbuf.at[sl, p // PAGES_PER_BLOCK, :, half])
        return [pltpu.make_async_copy(hbm.at[layer, page], dst, sem.at[a, sl])
                for a, (hbm, dst) in enumerate(zip(hbms, dsts))]

    def issue(sample, sl):
        for p in range(n_pages):
            for cp in page_copies(pt_ref[sample, p], p, sl):
                cp.start()

    def wait(sl):
        for p in range(n_pages):
            for cp in page_copies(0, p, sl):
                cp.wait()

    @pl.when(step == 0)
    def _():
        issue(0, 0)

    wait(slot)
    issue(jnp.minimum(step + 1, last), 1 - slot)

    q_sb = qs_ref[0]
    q_lat = qc_ref[0, :, :MLA_KV_RANK]
    q_rope = qc_ref[0, :, MLA_KV_RANK:MLA_KV_RANK + MLA_ROPE_DIM]
    kn = knew_ref[0].astype(F32)
    s_new = jnp.sum(qc_ref[0].astype(F32) * kn, axis=-1, keepdims=True)
    state = (jnp.zeros((SB_HEADS, 1), F32), jnp.zeros((SB_HEADS, LANES), F32), s_new,
             jnp.ones((MLA_HEADS, 1), F32), jnp.broadcast_to(kn[:, :MLA_KV_RANK], (MLA_HEADS, MLA_KV_RANK)))
    for ch in reversed(range(n_pages // DEC_PAGES)):
        blocks = pl.ds(ch * DEC_BLOCKS, DEC_BLOCKS)
        pages = pl.ds(ch * DEC_PAGES, DEC_PAGES)
        state = _paged_chunk(q_sb, q_lat, q_rope, u_tri, kbuf.at[slot, blocks], vbuf.at[slot, pages],
                             cbuf.at[slot, pages], rbuf.at[slot, blocks], state)
    _, sacc, _, l, macc = state
    lane = lax.broadcasted_iota(jnp.int32, (SB_GROUP, LANES), 1)
    sbo_ref[0] = jnp.where(lane < SB_HEAD_DIM, sacc[:SB_GROUP], sacc[SB_GROUP:]).astype(sbo_ref.dtype)
    mlat_ref[0] = (macc / l).astype(mlat_ref.dtype)

    _sb_prompt_body(q_ref, k_ref, v_ref, u_tri, o_ref, i, tq=tq, tk=tk)

    @pl.when(step == last)
    def _():
        wait(1 - slot)


def _sb_and_paged(sbq, kb, vb, page_table, q_sb, qcat, knew, ck, cv, cc, cr, *, layer, tq):
    B, S, _ = sbq.shape
    DB, n_pages = page_table.shape
    nq = S // tq
    assert DB == B * SB_GROUP * nq and n_pages % DEC_PAGES == 0 and KEY_TILE == DEC_BLOCK
    n_blocks = n_pages // PAGES_PER_BLOCK
    step = lambda b, p, i: (b * SB_GROUP + p) * nq + i
    grid_spec = pltpu.PrefetchScalarGridSpec(
        num_scalar_prefetch=1,
        grid=(B, SB_GROUP, nq),
        in_specs=[
            pl.BlockSpec((1, tq, 2 * SLOT), lambda b, p, i, pt: (b, i, p)),
            pl.BlockSpec((1, S, LANES), lambda b, p, i, pt: (b, 0, 0)),
            pl.BlockSpec((1, S, LANES), lambda b, p, i, pt: (b, 0, 0)),
            pl.BlockSpec((KEY_TILE, KEY_TILE), lambda b, p, i, pt: (0, 0)),
            pl.BlockSpec((1, SB_HEADS, SLOT), lambda b, p, i, pt: (step(b, p, i), 0, 0)),
            pl.BlockSpec((1, MLA_HEADS, MLA_QW), lambda b, p, i, pt: (step(b, p, i), 0, 0)),
            pl.BlockSpec((1, 1, 2 * LANES), lambda b, p, i, pt: (step(b, p, i), 0, 0)),
            pl.BlockSpec(memory_space=pl.ANY),
            pl.BlockSpec(memory_space=pl.ANY),
            pl.BlockSpec(memory_space=pl.ANY),
            pl.BlockSpec(memory_space=pl.ANY),
        ],
        out_specs=[
            pl.BlockSpec((1, tq, LANES), lambda b, p, i, pt: (b, i, p)),
            pl.BlockSpec((1, SB_GROUP, LANES), lambda b, p, i, pt: (step(b, p, i), 0, 0)),
            pl.BlockSpec((1, MLA_HEADS, MLA_KV_RANK), lambda b, p, i, pt: (step(b, p, i), 0, 0)),
        ],
        scratch_shapes=[
            pltpu.VMEM((2, n_blocks, LANES, DEC_BLOCK), F32),
            pltpu.VMEM((2, n_pages, LANES, PAGE_SIZE), F32),
            pltpu.VMEM((2, n_pages, PAGE_SIZE, MLA_KV_RANK), F32),
            pltpu.VMEM((2, n_blocks, MLA_ROPE_DIM, DEC_BLOCK), F32),
            pltpu.SemaphoreType.DMA((4, 2)),
        ],
    )
    return pl.pallas_call(
        functools.partial(_sb_paged_kernel, layer=layer, n_pages=n_pages, tq=tq, tk=KEY_TILE),
        grid_spec=grid_spec,
        out_shape=[jax.ShapeDtypeStruct((B, S, SB_WIDTH), BF16),
                   jax.ShapeDtypeStruct((DB, SB_GROUP, LANES), BF16),
                   jax.ShapeDtypeStruct((DB, MLA_HEADS, MLA_KV_RANK), BF16)],
        compiler_params=pltpu.CompilerParams(dimension_semantics=("arbitrary",) * 3,
                                             vmem_limit_bytes=VMEM_LIMIT),
        name="sb_prompt_paged",
    )(page_table, sbq, kb, vb, _tri(KEY_TILE), q_sb, qcat, knew, ck, cv, cc, cr)


def _block_diag(blocks):
    n = len(blocks)
    r, c = blocks[0].shape
    rows = []
    for i, blk in enumerate(blocks):
        rows.append(jnp.concatenate([jnp.zeros((r, c * i), blk.dtype), blk,
                                     jnp.zeros((r, c * (n - 1 - i)), blk.dtype)], axis=1))
    return jnp.concatenate(rows, axis=0)


def _rope_cols(w):
    half = MLA_ROPE_DIM // 2
    z = jnp.zeros((w.shape[0], LANES - MLA_ROPE_DIM), w.dtype)
    return jnp.concatenate([w, z], axis=1), jnp.concatenate([-w[:, half:], w[:, :half], z], axis=1)


def _prep_layer(l, norm_pre, norm_post, w_in, mla_q_norm, mla_kv_norm, mla_w_uq, mla_w_uk, mla_w_uv,
                pool_w, pool_scale, w_br_sb, w_br_mla, w_br_pool, w_out):
    w = w_in[l].astype(BF16)
    o = 0
    parts = {}
    for name, width in (("sbq", SB_WIDTH), ("k", LANES), ("v", LANES), ("gsb", SB_WIDTH), ("cq", MLA_Q_RANK),
                        ("ckv", MLA_KV_RANK), ("r", MLA_ROPE_DIM), ("gmla", MLA_WIDTH), ("pu", POOL_WIDTH),
                        ("pg", POOL_WIDTH), ("brg", 3 * D_MODEL)):
        parts[name] = w[:, o:o + width]
        o += width
    z64 = jnp.zeros((D_MODEL, SB_HEAD_DIM), BF16)
    sbq_cols, gsb_cols = [], []
    for s, hd in enumerate(SB_SLOT_HEADS):
        qh = parts["sbq"][:, hd * SB_HEAD_DIM:(hd + 1) * SB_HEAD_DIM]
        sbq_cols += [qh, z64] if s % 2 == 0 else [z64, qh]
        gsb_cols.append(parts["gsb"][:, hd * SB_HEAD_DIM:(hd + 1) * SB_HEAD_DIM])
    r_cols, rs_cols = _rope_cols(parts["r"])
    win = jnp.concatenate([parts["brg"]] + sbq_cols + gsb_cols +
                          [parts["gmla"], parts["pu"], parts["pg"], parts["cq"], parts["k"], parts["v"],
                           parts["ckv"], r_cols, rs_cols], axis=1)

    wuq = mla_w_uq[l].astype(BF16).reshape(MLA_Q_RANK, MLA_HEADS, MLA_NOPE_DIM + MLA_ROPE_DIM)
    wuqn = wuq[:, :, :MLA_NOPE_DIM].reshape(MLA_Q_RANK, MLA_HEADS * MLA_NOPE_DIM)
    rr = [_rope_cols(wuq[:, hd, MLA_NOPE_DIM:]) for hd in range(MLA_HEADS)]
    wuqr = jnp.concatenate([a for a, _ in rr], axis=1)
    wuqrs = jnp.concatenate([b for _, b in rr], axis=1)
    wuk = mla_w_uk[l].astype(BF16)
    wukbd = _block_diag([wuk[:, hd, :].T for hd in range(MLA_HEADS)])
    wuv = mla_w_uv[l].astype(BF16)
    wuvbd = _block_diag([wuv[:, hd, :] for hd in range(MLA_HEADS)])
    wsb = w_br_sb[l].astype(BF16)
    wsb = jnp.concatenate([wsb[hd * SB_HEAD_DIM:(hd + 1) * SB_HEAD_DIM] for hd in SB_SLOT_HEADS], axis=0)
    return dict(
        npre=norm_pre[l].reshape(1, D_MODEL), npost=norm_post[l].reshape(1, D_MODEL), win=win,
        qn=mla_q_norm[l].reshape(1, MLA_Q_RANK), kvn=mla_kv_norm[l].reshape(1, MLA_KV_RANK),
        wuqn=wuqn, wukbd=wukbd, wuqr=wuqr, wuqrs=wuqrs, wuvbd=wuvbd,
        poolw=pool_w[l].astype(BF16), pscale=pool_scale[l].reshape(1, POOL_WIDTH),
        wsb=wsb, wmla=w_br_mla[l].astype(BF16), wpool=w_br_pool[l].astype(BF16), wout=w_out[l].astype(BF16))


def _rope_tables(pos):
    half = MLA_ROPE_DIM // 2
    inv = ROPE_THETA ** (-jnp.arange(half, dtype=F32) / half)
    ang = pos.astype(F32)[:, None] * inv[None, :]
    z = jnp.zeros((pos.shape[0], LANES - MLA_ROPE_DIM), F32)
    cos, sin = jnp.cos(ang), jnp.sin(ang)
    return jnp.concatenate([cos, cos, z], axis=1), jnp.concatenate([sin, sin, z], axis=1)


def _layer(xp, xs, lw, rope_p, rope_s, state, page_table, ck, cv, cc, cr, layer, past):
    B, S, _ = xp.shape
    DB = xs.shape[0]
    tm = min(ROW_TILE, S)
    count = tuple(float(min(past + 1, w)) for w in POOL_WINDOWS)
    ap = _inproj(xp, lw, *rope_p, tm=tm)
    as_ = _inproj(xs.reshape(1, DB, D_MODEL), lw, *rope_s, tm=DB, state_t=jnp.transpose(state, (1, 0, 2)),
                  decode_count=count)
    q_sb = as_["sbq"].reshape(DB, SB_GROUP, 2, SLOT).transpose(0, 2, 1, 3).reshape(DB, SB_HEADS, SLOT)
    sbo_p, sbo_s, mlat_s = _sb_and_paged(ap["sbq"], ap["kb"], ap["vb"], page_table, q_sb,
                                         as_["qcat"].reshape(DB, MLA_HEADS, MLA_QW),
                                         as_["kcat"].reshape(DB, 1, 2 * LANES), ck, cv, cc, cr,
                                         layer=layer, tq=min(SB_Q_TILE, S))
    mlat_p = _mla_prompt(ap["qcat"], ap["kcat"], t=KEY_TILE)
    flat = lambda v: v.reshape(-1, v.shape[-1])
    yp = _merge(flat(xp), flat(sbo_p), flat(ap["gsb"]), flat(mlat_p), flat(ap["gmla"]), flat(ap["opool"]),
                flat(ap["brg"]), lw, tm=tm)
    ys = _merge(flat(xs), sbo_s.reshape(DB, SB_WIDTH), flat(as_["gsb"]), mlat_s.reshape(DB, -1),
                flat(as_["gmla"]), flat(as_["opool"]), flat(as_["brg"]), lw, tm=DB)
    return yp.reshape(B, S, D_MODEL), ys.reshape(DB, 1, D_MODEL), ap, as_


def kernel(x_prompt, x_sample, cache_sb_k, cache_sb_v, cache_mla_ckv, cache_mla_krope, state_pool, page_table,
           norm_pre, norm_post, w_in, mla_q_norm, mla_kv_norm, mla_w_uq, mla_w_uk, mla_w_uv,
           pool_w, pool_scale, w_br_sb, w_br_mla, w_br_pool, w_out):
    B, S, _ = x_prompt.shape
    DB, T, _ = x_sample.shape
    assert T == 1
    depth = w_in.shape[0]
    n_pages = page_table.shape[1]
    past = n_pages * PAGE_SIZE
    n_pool = cache_sb_k.shape[1]
    ck = jnp.transpose(cache_sb_k, (0, 1, 3, 4, 2)).reshape(depth, n_pool, LANES, PAGE_SIZE)
    cv = jnp.transpose(cache_sb_v, (0, 1, 3, 4, 2)).reshape(depth, n_pool, LANES, PAGE_SIZE)
    cr = jnp.transpose(cache_mla_krope, (0, 1, 3, 2))

    cos_p, sin_p = _rope_tables(jnp.arange(S, dtype=jnp.int32))
    cos_s, sin_s = _rope_tables(jnp.full((DB,), past, dtype=jnp.int32))

    xp, xs = x_prompt, x_sample
    outs = {n: [] for n in ("kp", "vp", "cp", "rp", "up", "ks", "vs", "cs", "rs", "us")}
    for l in range(depth):
        lw = _prep_layer(l, norm_pre, norm_post, w_in, mla_q_norm, mla_kv_norm, mla_w_uq, mla_w_uk, mla_w_uv,
                         pool_w, pool_scale, w_br_sb, w_br_mla, w_br_pool, w_out)
        xp, xs, a, a_s = _layer(xp, xs, lw, (cos_p, sin_p), (cos_s, sin_s), state_pool[l], page_table,
                                ck, cv, cache_mla_ckv, cr, l, past)
        outs["kp"].append(a["k32"].reshape(B, S, SB_KV_HEADS, SB_HEAD_DIM))
        outs["vp"].append(a["v32"].reshape(B, S, SB_KV_HEADS, SB_HEAD_DIM))
        outs["cp"].append(a["ckv32"])
        outs["rp"].append(a["kr32"])
        outs["up"].append(a["pst"][:, 16 - POOL_STATE_LEN:])
        a = a_s
        outs["ks"].append(a["k32"].reshape(DB, 1, SB_KV_HEADS, SB_HEAD_DIM))
        outs["vs"].append(a["v32"].reshape(DB, 1, SB_KV_HEADS, SB_HEAD_DIM))
        outs["cs"].append(a["ckv32"].reshape(DB, 1, MLA_KV_RANK))
        outs["rs"].append(a["kr32"].reshape(DB, 1, MLA_ROPE_DIM))
        outs["us"].append(jnp.concatenate([state_pool[l][:, 1:], a["pst"].reshape(DB, 1, POOL_WIDTH)], axis=1))
    st = lambda n: jnp.stack(outs[n])
    return (xp, xs, st("kp"), st("vp"), st("cp"), st("rp"), st("up"),
            st("ks"), st("vs"), st("cs"), st("rs"), st("us"))
```

```python
import functools
import math

import jax
import jax.numpy as jnp
from jax import lax
from jax.experimental import pallas as pl
from jax.experimental.pallas import tpu as pltpu

F32 = jnp.float32
BF16 = jnp.bfloat16

D_MODEL = 1024
PAGE_SIZE = 128
SB_HEADS = 8
SB_KV_HEADS = 2
SB_HEAD_DIM = 64
SB_GROUP = SB_HEADS // SB_KV_HEADS
SB_WIDTH = SB_HEADS * SB_HEAD_DIM
MLA_HEADS = 8
MLA_Q_RANK = 256
MLA_KV_RANK = 128
MLA_NOPE_DIM = 64
MLA_ROPE_DIM = 32
MLA_V_DIM = 64
MLA_WIDTH = MLA_HEADS * MLA_V_DIM
ROPE_THETA = 10000.0
POOL_WINDOWS = (2, 4, 8, 16)
POOL_WIDTH = 512
POOL_GROUP_DIM = 128
POOL_STATE_LEN = 15
RMS_EPS = 1e-6
LOG2E = math.log2(math.e)

LANES = 128
SLOT = 128
MLA_QW = 256
VMEM_LIMIT = 56 * 1024 * 1024
ROW_TILE = 512
KEY_TILE = 256
SB_Q_TILE = 512
MLA_TILE = 512

SB_SLOT_HEADS = tuple((s // 2) + SB_GROUP * (s % 2) for s in range(SB_HEADS))

C_BRG = 0
C_SBQ = C_BRG + 3 * D_MODEL
C_GSB = C_SBQ + SB_HEADS * SLOT
C_GMLA = C_GSB + SB_WIDTH
C_PU = C_GMLA + MLA_WIDTH
C_PG = C_PU + POOL_WIDTH
C_CQ = C_PG + POOL_WIDTH
C_K = C_CQ + MLA_Q_RANK
C_V = C_K + LANES
C_CKV = C_V + LANES
C_R = C_CKV + LANES
C_RS = C_R + LANES
IN_PAD = C_RS + LANES


def _dot(a, b):
    return jnp.dot(a, b, preferred_element_type=F32)


def _dot_nt(a, b):
    return lax.dot_general(a, b, (((1,), (1,)), ((), ())), preferred_element_type=F32)


def _sigmoid(x):
    return 1.0 / (1.0 + jnp.exp(-x))


def _rms(x, g):
    return x * lax.rsqrt(jnp.mean(x * x, axis=-1, keepdims=True) + RMS_EPS) * g


def _const_spec(shape):
    nd = len(shape)
    return pl.BlockSpec(shape, lambda *_: (0,) * nd, pipeline_mode=pl.Buffered(1))


def _inproj_kernel(*refs, tm, decode, decode_count):
    (x_ref, npre_ref, win_ref, qn_ref, wuqn_ref, wukbd_ref, wuqr_ref, wuqrs_ref, kvn_ref,
     cos_ref, sin_ref, poolw_ref, pscale_ref) = refs[:13]
    refs = refs[13:]
    if decode:
        state_ref, refs = refs[0], refs[1:]
    (k32_ref, v32_ref, ckv32_ref, kr32_ref, pst_ref, sbq_ref, kb_ref, vb_ref, qcat_ref, kcat_ref,
     gsb_ref, gmla_ref, opool_ref, brg_ref) = refs[:14]
    refs = refs[14:]

    x = x_ref[0]
    h = _rms(x, npre_ref[...]).astype(BF16)

    def proj(off, width):
        return _dot(h, win_ref[:, off:off + width])

    for c in range(0, 3 * D_MODEL, 512):
        brg_ref[0, :, c:c + 512] = _sigmoid(proj(C_BRG + c, 512)).astype(BF16)

    for c in range(0, SB_HEADS * SLOT, 512):
        sbq_ref[0, :, c:c + 512] = (proj(C_SBQ + c, 512) * (LOG2E / math.sqrt(SB_HEAD_DIM))).astype(BF16)
    k = proj(C_K, LANES)
    v = proj(C_V, LANES)
    k32_ref[0] = k
    v32_ref[0] = v
    kb_ref[0] = k.astype(BF16)
    vb_ref[0] = v.astype(BF16)
    g = proj(C_GSB, SB_WIDTH)
    gsb_ref[0] = (g * _sigmoid(g)).astype(BF16)
    g = proj(C_GMLA, MLA_WIDTH)
    gmla_ref[0] = (g * _sigmoid(g)).astype(BF16)

    cos = cos_ref[...]
    sin = sin_ref[...]
    ckv = _rms(proj(C_CKV, LANES), kvn_ref[...])
    krot = proj(C_R, LANES) * cos + proj(C_RS, LANES) * sin
    ckv32_ref[0] = ckv
    kr32_ref[0] = krot[:, :MLA_ROPE_DIM]
    kcat_ref[0, :, :LANES] = ckv.astype(BF16)
    kcat_ref[0, :, LANES:] = krot.astype(BF16)

    cqn = _rms(proj(C_CQ, MLA_Q_RANK), qn_ref[...]).astype(BF16)
    qnope = _dot(cqn, wuqn_ref[...]).astype(BF16)
    qscale = LOG2E / math.sqrt(MLA_NOPE_DIM + MLA_ROPE_DIM)
    for hd in range(MLA_HEADS):
        lo = hd * LANES
        qlat = _dot(qnope, wukbd_ref[:, lo:lo + LANES])
        qrot = _dot(cqn, wuqr_ref[:, lo:lo + LANES]) * cos + _dot(cqn, wuqrs_ref[:, lo:lo + LANES]) * sin
        qcat_ref[0, :, hd * MLA_QW:hd * MLA_QW + LANES] = (qlat * qscale).astype(BF16)
        qcat_ref[0, :, hd * MLA_QW + LANES:(hd + 1) * MLA_QW] = (qrot * qscale).astype(BF16)

    u = proj(C_PU, POOL_WIDTH)
    sums = []
    if decode:
        pst_ref[0] = u
        run = u
        nxt = POOL_STATE_LEN - 1
        for gi, w in enumerate(POOL_WINDOWS):
            while nxt >= POOL_STATE_LEN - (w - 1):
                run = run + state_ref[nxt]
                nxt -= 1
            sums.append(run[:, gi * LANES:(gi + 1) * LANES] * (1.0 / decode_count[gi]))
    else:
        prev_ref = refs[0]
        i = pl.program_id(1)

        @pl.when(i == 0)
        def _():
            prev_ref[...] = jnp.zeros_like(prev_ref)

        ext = jnp.concatenate([prev_ref[...], u], axis=0)
        prev_ref[...] = u[tm - 16:, :]
        pst_ref[0] = u[tm - 16:, :]
        pos = i * tm + lax.broadcasted_iota(jnp.int32, (tm, 1), 0)
        run = ext
        span = 1
        for gi, w in enumerate(POOL_WINDOWS):
            while span < w:
                run = run + pltpu.roll(run, span, 0)
                span *= 2
            cnt = jnp.minimum(pos + 1, w).astype(F32)
            sums.append(run[16:, gi * LANES:(gi + 1) * LANES] / cnt)
    pg = proj(C_PG, POOL_WIDTH)
    pgate = pg * _sigmoid(pg)
    for gi in range(len(POOL_WINDOWS)):
        sl = slice(gi * LANES, (gi + 1) * LANES)
        d = (sums[gi] - u[:, sl]).astype(BF16)
        y = _dot(d, poolw_ref[gi]) * pscale_ref[:, sl]
        opool_ref[0, :, sl] = (y * pgate[:, sl]).astype(BF16)


def _inproj(x, lw, cos, sin, *, tm, state_t=None, decode_count=None):
    B, S, _ = x.shape
    decode = state_t is not None
    nt = S // tm
    row = lambda w: pl.BlockSpec((1, tm, w), lambda b, i: (b, i, 0))
    in_specs = [
        row(D_MODEL),
        _const_spec((1, D_MODEL)),
        _const_spec((D_MODEL, IN_PAD)),
        _const_spec((1, MLA_Q_RANK)),
        _const_spec((MLA_Q_RANK, MLA_HEADS * MLA_NOPE_DIM)),
        _const_spec((MLA_HEADS * MLA_NOPE_DIM, MLA_HEADS * LANES)),
        _const_spec((MLA_Q_RANK, MLA_HEADS * LANES)),
        _const_spec((MLA_Q_RANK, MLA_HEADS * LANES)),
        _const_spec((1, MLA_KV_RANK)),
        pl.BlockSpec((tm, LANES), lambda b, i: (i, 0)),
        pl.BlockSpec((tm, LANES), lambda b, i: (i, 0)),
        _const_spec((len(POOL_WINDOWS), POOL_GROUP_DIM, POOL_GROUP_DIM)),
        _const_spec((1, POOL_WIDTH)),
    ]
    args = [x, lw["npre"], lw["win"], lw["qn"], lw["wuqn"], lw["wukbd"], lw["wuqr"], lw["wuqrs"], lw["kvn"],
            cos, sin, lw["poolw"], lw["pscale"]]
    if decode:
        in_specs.append(_const_spec(state_t.shape))
        args.append(state_t)
        pst_shape, pst_spec = (B, S, POOL_WIDTH), row(POOL_WIDTH)
        scratch = []
    else:
        pst_shape = (B, 16, POOL_WIDTH)
        pst_spec = pl.BlockSpec((1, 16, POOL_WIDTH), lambda b, i: (b, 0, 0))
        scratch = [pltpu.VMEM((16, POOL_WIDTH), F32)]
    outs = [("k32", LANES, F32), ("v32", LANES, F32), ("ckv32", LANES, F32), ("kr32", MLA_ROPE_DIM, F32),
            ("pst", None, F32), ("sbq", SB_HEADS * SLOT, BF16), ("kb", LANES, BF16), ("vb", LANES, BF16),
            ("qcat", MLA_HEADS * MLA_QW, BF16), ("kcat", 2 * LANES, BF16), ("gsb", SB_WIDTH, BF16),
            ("gmla", MLA_WIDTH, BF16), ("opool", POOL_WIDTH, BF16), ("brg", 3 * D_MODEL, BF16)]
    out_shape, out_specs = [], []
    for name, w, dt in outs:
        if name == "pst":
            out_shape.append(jax.ShapeDtypeStruct(pst_shape, dt))
            out_specs.append(pst_spec)
        else:
            out_shape.append(jax.ShapeDtypeStruct((B, S, w), dt))
            out_specs.append(row(w))
    res = pl.pallas_call(
        functools.partial(_inproj_kernel, tm=tm, decode=decode, decode_count=decode_count),
        grid=(B, nt),
        in_specs=in_specs,
        out_specs=out_specs,
        out_shape=out_shape,
        scratch_shapes=scratch,
        compiler_params=pltpu.CompilerParams(dimension_semantics=("arbitrary", "arbitrary"),
                                             vmem_limit_bytes=VMEM_LIMIT),
        name="inproj_decode" if decode else "inproj_prompt",
    )(*args)
    return dict(zip([o[0] for o in outs], res))


def _softplus2(z):
    return jnp.maximum(z, 0.0) + jnp.log(1.0 + jnp.exp2(-jnp.abs(z))) * LOG2E


def _suffix_sums(x, u_tri):
    return _dot(x.astype(BF16), u_tri)


def _sb_weights(z, u_tri, carry, valid=None):
    sp = _softplus2(z)
    if valid is not None:
        sp = jnp.where(valid, sp, 0.0)
    incl = _suffix_sums(sp, u_tri)
    w = jnp.exp2(z - incl - carry)
    if valid is not None:
        w = jnp.where(valid, w, 0.0)
    return w.astype(BF16), carry + incl[:, 0:1]


def _sb_block(q, kblk, vblk, u_tri, carry, valid=None):
    w, carry = _sb_weights(_dot_nt(q, kblk), u_tri, carry, valid)
    return _dot(w, vblk), carry


def _sb_prompt_body(q_ref, k_ref, v_ref, u_tri, o_ref, i, *, tq, tk):
    nsub = tq // tk
    q = jnp.concatenate([q_ref[0, :, :SLOT], q_ref[0, :, SLOT:]], axis=0)
    rows = 2 * tq
    qpos = lax.broadcasted_iota(jnp.int32, (rows, tk), 0) & (tq - 1)
    kpos = lax.broadcasted_iota(jnp.int32, (rows, tk), 1)

    def block(start, carry, valid=None):
        start = pl.multiple_of(start, tk)
        return _sb_block(q, k_ref[0, pl.ds(start, tk), :], v_ref[0, pl.ds(start, tk), :], u_tri, carry, valid)

    acc = jnp.zeros((rows, LANES), F32)
    carry = jnp.zeros((rows, 1), F32)
    for sub in reversed(range(nsub)):
        a, carry = block(i * tq + sub * tk, carry, kpos + sub * tk < qpos)
        acc = acc + a

    def walk(nblk, first):
        def body(t, st):
            acc, carry = st
            for n in range(nblk):
                a, carry = block(i * tq - (first(t) + n + 1) * tk, carry)
                acc = acc + a
            return acc, carry
        return body

    odd = i % 2
    st = lax.fori_loop(0, odd, walk(nsub, lambda t: 0), (acc, carry))
    acc, _ = lax.fori_loop(0, i // 2, walk(2 * nsub, lambda t: nsub * (odd + 2 * t)), st)
    lane = lax.broadcasted_iota(jnp.int32, (tq, LANES), 1)
    o_ref[0] = jnp.where(lane < SB_HEAD_DIM, acc[:tq], acc[tq:]).astype(o_ref.dtype)


def _tri(t):
    j = lax.broadcasted_iota(jnp.int32, (t, t), 0)
    k = lax.broadcasted_iota(jnp.int32, (t, t), 1)
    return (j >= k).astype(BF16)


def _softmax_block(q, kc, m, l, acc, valid=None):
    s = _dot_nt(q, kc)
    if valid is not None:
        s = jnp.where(valid, s, -jnp.inf)
    m_new = jnp.maximum(m, jnp.max(s, axis=-1, keepdims=True))
    alpha = jnp.exp2(m - m_new)
    p = jnp.exp2(s - m_new)
    l = alpha * l + jnp.sum(p, axis=-1, keepdims=True)
    acc = alpha * acc + _dot(p.astype(BF16), kc[:, :MLA_KV_RANK])
    return m_new, l, acc


def _mla_prompt_kernel(q_ref, kc_ref, o_ref, *, t):
    i = pl.program_id(1)
    q = jnp.concatenate([q_ref[0, :, hd * MLA_QW:(hd + 1) * MLA_QW] for hd in range(MLA_HEADS)], axis=0)
    rows = MLA_HEADS * t
    qpos = lax.broadcasted_iota(jnp.int32, (rows, t), 0) & (t - 1)
    kpos = lax.broadcasted_iota(jnp.int32, (rows, t), 1)
    start = pl.multiple_of(i * t, t)
    st = _softmax_block(q, kc_ref[0, pl.ds(start, t), :], jnp.full((rows, 1), -jnp.inf, F32),
                        jnp.zeros((rows, 1), F32), jnp.zeros((rows, MLA_KV_RANK), F32), kpos <= qpos)

    def body(j, st):
        s0 = pl.multiple_of(j * t, t)
        return _softmax_block(q, kc_ref[0, pl.ds(s0, t), :], *st)

    m, l, acc = lax.fori_loop(0, i, body, st)
    out = (acc / l).astype(o_ref.dtype)
    for hd in range(MLA_HEADS):
        o_ref[0, :, hd * MLA_KV_RANK:(hd + 1) * MLA_KV_RANK] = out[hd * t:(hd + 1) * t]


def _mla_prompt(qcat, kcat, *, t):
    B, S, _ = qcat.shape
    return pl.pallas_call(
        functools.partial(_mla_prompt_kernel, t=t),
        grid=(B, S // t),
        in_specs=[
            pl.BlockSpec((1, t, MLA_HEADS * MLA_QW), lambda b, i: (b, i, 0)),
            pl.BlockSpec((1, S, 2 * LANES), lambda b, i: (b, 0, 0)),
        ],
        out_specs=pl.BlockSpec((1, t, MLA_HEADS * MLA_KV_RANK), lambda b, i: (b, i, 0)),
        out_shape=jax.ShapeDtypeStruct((B, S, MLA_HEADS * MLA_KV_RANK), BF16),
        compiler_params=pltpu.CompilerParams(dimension_semantics=("arbitrary",) * 2,
                                             vmem_limit_bytes=VMEM_LIMIT),
        name="mla_prompt",
    )(qcat, kcat)


def _merge_kernel(x_ref, sbo_ref, gsb_ref, mlat_ref, gmla_ref, opool_ref, brg_ref,
                  wsb_ref, wuv_ref, wmla_ref, wpool_ref, wout_ref, npost_ref, y_ref):
    o_sb = (sbo_ref[...].astype(F32) * gsb_ref[...].astype(F32)).astype(BF16)
    merged = brg_ref[:, :D_MODEL].astype(F32) * _dot(o_sb, wsb_ref[...])
    mla_o = _dot(mlat_ref[...], wuv_ref[...])
    o_mla = (mla_o * gmla_ref[...].astype(F32)).astype(BF16)
    merged += brg_ref[:, D_MODEL:2 * D_MODEL].astype(F32) * _dot(o_mla, wmla_ref[...])
    merged += brg_ref[:, 2 * D_MODEL:].astype(F32) * _dot(opool_ref[...], wpool_ref[...])
    o = _dot(merged.astype(BF16), wout_ref[...])
    y_ref[...] = x_ref[...] + _rms(o, npost_ref[...])


def _merge(x2, sbo, gsb, mlat, gmla, opool, brg, lw, *, tm):
    M = x2.shape[0]
    row = lambda w: pl.BlockSpec((tm, w), lambda i: (i, 0))
    return pl.pallas_call(
        _merge_kernel,
        grid=(M // tm,),
        in_specs=[row(D_MODEL), row(SB_WIDTH), row(SB_WIDTH), row(MLA_HEADS * MLA_KV_RANK), row(MLA_WIDTH),
                  row(POOL_WIDTH), row(3 * D_MODEL),
                  _const_spec((SB_WIDTH, D_MODEL)), _const_spec((MLA_HEADS * MLA_KV_RANK, MLA_WIDTH)),
                  _const_spec((MLA_WIDTH, D_MODEL)), _const_spec((POOL_WIDTH, D_MODEL)),
                  _const_spec((D_MODEL, D_MODEL)), _const_spec((1, D_MODEL))],
        out_specs=row(D_MODEL),
        out_shape=jax.ShapeDtypeStruct((M, D_MODEL), F32),
        compiler_params=pltpu.CompilerParams(dimension_semantics=("arbitrary",),
                                             vmem_limit_bytes=VMEM_LIMIT),
        name="merge",
    )(x2, sbo, gsb, mlat, gmla, opool, brg, lw["wsb"], lw["wuvbd"], lw["wmla"], lw["wpool"], lw["wout"],
      lw["npost"])


DEC_BLOCK = 256
DEC_PAGES = 16
PAGES_PER_BLOCK = DEC_BLOCK // PAGE_SIZE
DEC_BLOCKS = DEC_PAGES // PAGES_PER_BLOCK


def _paged_chunk(q_sb, q_lat, q_rope, u_tri, kb, vb, cb, rb, state):
    carry, sacc, m, l, macc = state
    blocks = range(DEC_BLOCKS)
    pair = lambda ref, n: ref[PAGES_PER_BLOCK * n:PAGES_PER_BLOCK * (n + 1)].reshape(DEC_BLOCK, LANES).astype(BF16)
    cpairs = [pair(cb, n) for n in blocks]
    z = jnp.concatenate([_dot(q_sb, kb[n].astype(BF16)) for n in blocks], axis=1)
    s = jnp.concatenate([_dot_nt(q_lat, cpairs[n]) + _dot(q_rope, rb[n].astype(BF16)) for n in blocks], axis=1)

    sp = _softplus2(z)
    lanes = [slice(n * DEC_BLOCK, (n + 1) * DEC_BLOCK) for n in blocks]
    incl = _suffix_sums(jnp.concatenate([sp[:, ln] for ln in lanes], axis=0), u_tri)
    for n in reversed(blocks):
        inc = incl[n * SB_HEADS:(n + 1) * SB_HEADS]
        w = jnp.exp2(z[:, lanes[n]] - inc - carry)
        carry = carry + inc[:, 0:1]
        w2 = jnp.concatenate([w[:, :PAGE_SIZE], w[:, PAGE_SIZE:]], axis=0).astype(BF16)
        r = _dot_nt(w2, pair(vb, n))
        sacc = sacc + r[:SB_HEADS, :LANES] + r[SB_HEADS:, LANES:]

    m_new = jnp.maximum(m, jnp.max(s, axis=-1, keepdims=True))
    alpha = jnp.exp2(m - m_new)
    p = jnp.exp2(s - m_new)
    l = alpha * l + jnp.sum(p, axis=-1, keepdims=True)
    p = p.astype(BF16)
    macc = alpha * macc
    for n in blocks:
        macc = macc + _dot(p[:, lanes[n]], cpairs[n])
    return carry, sacc, m_new, l, macc


def _sb_paged_kernel(pt_ref, q_ref, k_ref, v_ref, u_ref, qs_ref, qc_ref, knew_ref, ck_hbm, cv_hbm, cc_hbm, cr_hbm,
                     o_ref, sbo_ref, mlat_ref, k0, v0, c0, r0, k1, v1, c1, r1, sem, *, layer, n_pages, n_steps, tq, tk):
    i = pl.program_id(2)
    step = (pl.program_id(0) * pl.num_programs(1) + pl.program_id(1)) * pl.num_programs(2) + i
    last = n_steps - 1
    u_tri = u_ref[...]
    hbms = (ck_hbm, cv_hbm, cc_hbm, cr_hbm)
    slots = ((k0, v0, c0, r0), (k1, v1, c1, r1))

    def page_copies(page, p, sl):
        kb, vb, cb, rb = slots[sl]
        half = pl.ds((p % PAGES_PER_BLOCK) * PAGE_SIZE, PAGE_SIZE)
        dsts = (kb.at[p // PAGES_PER_BLOCK, :, half], vb.at[p], cb.at[p], rb.at[p // PAGES_PER_BLOCK, :, half])
        return [pltpu.make_async_copy(hbm.at[layer, page], dst, sem.at[a, sl])
                for a, (hbm, dst) in enumerate(zip(hbms, dsts))]

    def issue(sample, sl):
        for p in range(n_pages):
            for cp in page_copies(pt_ref[sample, p], p, sl):
                cp.start()

    def wait(sl):
        for p in range(n_pages):
            for cp in page_copies(0, p, sl):
                cp.wait()

    def sample_attention(cur):
        wait(cur)
        issue(jnp.minimum(step + 1, last), 1 - cur)
        kb, vb, cb, rb = slots[cur]
        q_sb = qs_ref[0]
        q_lat = qc_ref[0, :, :MLA_KV_RANK]
        q_rope = qc_ref[0, :, MLA_KV_RANK:MLA_KV_RANK + MLA_ROPE_DIM]
        kn = knew_ref[0].astype(F32)
        s_new = jnp.sum(qc_ref[0].astype(F32) * kn, axis=-1, keepdims=True)
        state = (jnp.zeros((SB_HEADS, 1), F32), jnp.zeros((SB_HEADS, LANES), F32), s_new,
                 jnp.ones((MLA_HEADS, 1), F32), jnp.broadcast_to(kn[:, :MLA_KV_RANK], (MLA_HEADS, MLA_KV_RANK)))
        for ch in reversed(range(n_pages // DEC_PAGES)):
            blocks = pl.ds(ch * DEC_BLOCKS, DEC_BLOCKS)
            pages = pl.ds(ch * DEC_PAGES, DEC_PAGES)
            state = _paged_chunk(q_sb, q_lat, q_rope, u_tri, kb.at[blocks], vb.at[pages], cb.at[pages],
                                 rb.at[blocks], state)
        _, sacc, _, l, macc = state
        lane = lax.broadcasted_iota(jnp.int32, (SB_GROUP, LANES), 1)
        sbo_ref[0] = jnp.where(lane < SB_HEAD_DIM, sacc[:SB_GROUP], sacc[SB_GROUP:]).astype(sbo_ref.dtype)
        mlat_ref[0] = (macc / l).astype(mlat_ref.dtype)

    @pl.when(step == 0)
    def _():
        issue(0, 0)

    for cur in range(2):
        pl.when(step % 2 == cur)(functools.partial(sample_attention, cur))

    _sb_prompt_body(q_ref, k_ref, v_ref, u_tri, o_ref, i, tq=tq, tk=tk)

    @pl.when(step == last)
    def _():
        wait(1 - last % 2)


def _sb_and_paged(sbq, kb, vb, page_table, q_sb, qcat, knew, ck, cv, cc, cr, *, layer, tq):
    B, S, _ = sbq.shape
    DB, n_pages = page_table.shape
    nq = S // tq
    assert DB == B * SB_GROUP * nq and n_pages % DEC_PAGES == 0 and KEY_TILE == DEC_BLOCK
    n_blocks = n_pages // PAGES_PER_BLOCK
    step = lambda b, p, i: (b * SB_GROUP + p) * nq + i
    grid_spec = pltpu.PrefetchScalarGridSpec(
        num_scalar_prefetch=1,
        grid=(B, SB_GROUP, nq),
        in_specs=[
            pl.BlockSpec((1, tq, 2 * SLOT), lambda b, p, i, pt: (b, i, p)),
            pl.BlockSpec((1, S, LANES), lambda b, p, i, pt: (b, 0, 0)),
            pl.BlockSpec((1, S, LANES), lambda b, p, i, pt: (b, 0, 0)),
            pl.BlockSpec((KEY_TILE, KEY_TILE), lambda b, p, i, pt: (0, 0)),
            pl.BlockSpec((1, SB_HEADS, SLOT), lambda b, p, i, pt: (step(b, p, i), 0, 0)),
            pl.BlockSpec((1, MLA_HEADS, MLA_QW), lambda b, p, i, pt: (step(b, p, i), 0, 0)),
            pl.BlockSpec((1, 1, 2 * LANES), lambda b, p, i, pt: (step(b, p, i), 0, 0)),
            pl.BlockSpec(memory_space=pl.ANY),
            pl.BlockSpec(memory_space=pl.ANY),
            pl.BlockSpec(memory_space=pl.ANY),
            pl.BlockSpec(memory_space=pl.ANY),
        ],
        out_specs=[
            pl.BlockSpec((1, tq, LANES), lambda b, p, i, pt: (b, i, p)),
            pl.BlockSpec((1, SB_GROUP, LANES), lambda b, p, i, pt: (step(b, p, i), 0, 0)),
            pl.BlockSpec((1, MLA_HEADS, MLA_KV_RANK), lambda b, p, i, pt: (step(b, p, i), 0, 0)),
        ],
        scratch_shapes=2 * [
            pltpu.VMEM((n_blocks, LANES, DEC_BLOCK), F32),
            pltpu.VMEM((n_pages, LANES, PAGE_SIZE), F32),
            pltpu.VMEM((n_pages, PAGE_SIZE, MLA_KV_RANK), F32),
            pltpu.VMEM((n_blocks, MLA_ROPE_DIM, DEC_BLOCK), F32),
        ] + [pltpu.SemaphoreType.DMA((4, 2))],
    )
    return pl.pallas_call(
        functools.partial(_sb_paged_kernel, layer=layer, n_pages=n_pages, n_steps=DB, tq=tq, tk=KEY_TILE),
        grid_spec=grid_spec,
        out_shape=[jax.ShapeDtypeStruct((B, S, SB_WIDTH), BF16),
                   jax.ShapeDtypeStruct((DB, SB_GROUP, LANES), BF16),
                   jax.ShapeDtypeStruct((DB, MLA_HEADS, MLA_KV_RANK), BF16)],
        compiler_params=pltpu.CompilerParams(dimension_semantics=("arbitrary",) * 3,
                                             vmem_limit_bytes=VMEM_LIMIT),
        name="sb_prompt_paged",
    )(page_table, sbq, kb, vb, _tri(KEY_TILE), q_sb, qcat, knew, ck, cv, cc, cr)


def _block_diag(blocks):
    n = len(blocks)
    r, c = blocks[0].shape
    rows = []
    for i, blk in enumerate(blocks):
        rows.append(jnp.concatenate([jnp.zeros((r, c * i), blk.dtype), blk,
                                     jnp.zeros((r, c * (n - 1 - i)), blk.dtype)], axis=1))
    return jnp.concatenate(rows, axis=0)


def _rope_cols(w):
    half = MLA_ROPE_DIM // 2
    z = jnp.zeros((w.shape[0], LANES - MLA_ROPE_DIM), w.dtype)
    return jnp.concatenate([w, z], axis=1), jnp.concatenate([-w[:, half:], w[:, :half], z], axis=1)


def _prep_layer(l, norm_pre, norm_post, w_in, mla_q_norm, mla_kv_norm, mla_w_uq, mla_w_uk, mla_w_uv,
                pool_w, pool_scale, w_br_sb, w_br_mla, w_br_pool, w_out):
    w = w_in[l].astype(BF16)
    o = 0
    parts = {}
    for name, width in (("sbq", SB_WIDTH), ("k", LANES), ("v", LANES), ("gsb", SB_WIDTH), ("cq", MLA_Q_RANK),
                        ("ckv", MLA_KV_RANK), ("r", MLA_ROPE_DIM), ("gmla", MLA_WIDTH), ("pu", POOL_WIDTH),
                        ("pg", POOL_WIDTH), ("brg", 3 * D_MODEL)):
        parts[name] = w[:, o:o + width]
        o += width
    z64 = jnp.zeros((D_MODEL, SB_HEAD_DIM), BF16)
    sbq_cols, gsb_cols = [], []
    for s, hd in enumerate(SB_SLOT_HEADS):
        qh = parts["sbq"][:, hd * SB_HEAD_DIM:(hd + 1) * SB_HEAD_DIM]
        sbq_cols += [qh, z64] if s % 2 == 0 else [z64, qh]
        gsb_cols.append(parts["gsb"][:, hd * SB_HEAD_DIM:(hd + 1) * SB_HEAD_DIM])
    r_cols, rs_cols = _rope_cols(parts["r"])
    win = jnp.concatenate([parts["brg"]] + sbq_cols + gsb_cols +
                          [parts["gmla"], parts["pu"], parts["pg"], parts["cq"], parts["k"], parts["v"],
                           parts["ckv"], r_cols, rs_cols], axis=1)

    wuq = mla_w_uq[l].astype(BF16).reshape(MLA_Q_RANK, MLA_HEADS, MLA_NOPE_DIM + MLA_ROPE_DIM)
    wuqn = wuq[:, :, :MLA_NOPE_DIM].reshape(MLA_Q_RANK, MLA_HEADS * MLA_NOPE_DIM)
    rr = [_rope_cols(wuq[:, hd, MLA_NOPE_DIM:]) for hd in range(MLA_HEADS)]
    wuqr = jnp.concatenate([a for a, _ in rr], axis=1)
    wuqrs = jnp.concatenate([b for _, b in rr], axis=1)
    wuk = mla_w_uk[l].astype(BF16)
    wukbd = _block_diag([wuk[:, hd, :].T for hd in range(MLA_HEADS)])
    wuv = mla_w_uv[l].astype(BF16)
    wuvbd = _block_diag([wuv[:, hd, :] for hd in range(MLA_HEADS)])
    wsb = w_br_sb[l].astype(BF16)
    wsb = jnp.concatenate([wsb[hd * SB_HEAD_DIM:(hd + 1) * SB_HEAD_DIM] for hd in SB_SLOT_HEADS], axis=0)
    return dict(
        npre=norm_pre[l].reshape(1, D_MODEL), npost=norm_post[l].reshape(1, D_MODEL), win=win,
        qn=mla_q_norm[l].reshape(1, MLA_Q_RANK), kvn=mla_kv_norm[l].reshape(1, MLA_KV_RANK),
        wuqn=wuqn, wukbd=wukbd, wuqr=wuqr, wuqrs=wuqrs, wuvbd=wuvbd,
        poolw=pool_w[l].astype(BF16), pscale=pool_scale[l].reshape(1, POOL_WIDTH),
        wsb=wsb, wmla=w_br_mla[l].astype(BF16), wpool=w_br_pool[l].astype(BF16), wout=w_out[l].astype(BF16))


def _rope_tables(pos):
    half = MLA_ROPE_DIM // 2
    inv = ROPE_THETA ** (-jnp.arange(half, dtype=F32) / half)
    ang = pos.astype(F32)[:, None] * inv[None, :]
    z = jnp.zeros((pos.shape[0], LANES - MLA_ROPE_DIM), F32)
    cos, sin = jnp.cos(ang), jnp.sin(ang)
    return jnp.concatenate([cos, cos, z], axis=1), jnp.concatenate([sin, sin, z], axis=1)


def _layer(xp, xs, lw, rope_p, rope_s, state, page_table, ck, cv, cc, cr, layer, past):
    B, S, _ = xp.shape
    DB = xs.shape[0]
    tm = min(ROW_TILE, S)
    count = tuple(float(min(past + 1, w)) for w in POOL_WINDOWS)
    ap = _inproj(xp, lw, *rope_p, tm=tm)
    as_ = _inproj(xs.reshape(1, DB, D_MODEL), lw, *rope_s, tm=DB, state_t=jnp.transpose(state, (1, 0, 2)),
                  decode_count=count)
    q_sb = as_["sbq"].reshape(DB, SB_GROUP, 2, SLOT).transpose(0, 2, 1, 3).reshape(DB, SB_HEADS, SLOT)
    sbo_p, sbo_s, mlat_s = _sb_and_paged(ap["sbq"], ap["kb"], ap["vb"], page_table, q_sb,
                                         as_["qcat"].reshape(DB, MLA_HEADS, MLA_QW),
                                         as_["kcat"].reshape(DB, 1, 2 * LANES), ck, cv, cc, cr,
                                         layer=layer, tq=min(SB_Q_TILE, S))
    mlat_p = _mla_prompt(ap["qcat"], ap["kcat"], t=min(MLA_TILE, S))
    flat = lambda v: v.reshape(-1, v.shape[-1])
    yp = _merge(flat(xp), flat(sbo_p), flat(ap["gsb"]), flat(mlat_p), flat(ap["gmla"]), flat(ap["opool"]),
                flat(ap["brg"]), lw, tm=tm)
    ys = _merge(flat(xs), sbo_s.reshape(DB, SB_WIDTH), flat(as_["gsb"]), mlat_s.reshape(DB, -1),
                flat(as_["gmla"]), flat(as_["opool"]), flat(as_["brg"]), lw, tm=DB)
    return yp.reshape(B, S, D_MODEL), ys.reshape(DB, 1, D_MODEL), ap, as_


def kernel(x_prompt, x_sample, cache_sb_k, cache_sb_v, cache_mla_ckv, cache_mla_krope, state_pool, page_table,
           norm_pre, norm_post, w_in, mla_q_norm, mla_kv_norm, mla_w_uq, mla_w_uk, mla_w_uv,
           pool_w, pool_scale, w_br_sb, w_br_mla, w_br_pool, w_out):
    B, S, _ = x_prompt.shape
    DB, T, _ = x_sample.shape
    assert T == 1
    depth = w_in.shape[0]
    n_pages = page_table.shape[1]
    past = n_pages * PAGE_SIZE
    n_pool = cache_sb_k.shape[1]
    ck = jnp.transpose(cache_sb_k, (0, 1, 3, 4, 2)).reshape(depth, n_pool, LANES, PAGE_SIZE)
    cv = jnp.transpose(cache_sb_v, (0, 1, 3, 4, 2)).reshape(depth, n_pool, LANES, PAGE_SIZE)
    cr = jnp.transpose(cache_mla_krope, (0, 1, 3, 2))

    cos_p, sin_p = _rope_tables(jnp.arange(S, dtype=jnp.int32))
    cos_s, sin_s = _rope_tables(jnp.full((DB,), past, dtype=jnp.int32))

    xp, xs = x_prompt, x_sample
    outs = {n: [] for n in ("kp", "vp", "cp", "rp", "up", "ks", "vs", "cs", "rs", "us")}
    for l in range(depth):
        lw = _prep_layer(l, norm_pre, norm_post, w_in, mla_q_norm, mla_kv_norm, mla_w_uq, mla_w_uk, mla_w_uv,
                         pool_w, pool_scale, w_br_sb, w_br_mla, w_br_pool, w_out)
        xp, xs, a, a_s = _layer(xp, xs, lw, (cos_p, sin_p), (cos_s, sin_s), state_pool[l], page_table,
                                ck, cv, cache_mla_ckv, cr, l, past)
        outs["kp"].append(a["k32"].reshape(B, S, SB_KV_HEADS, SB_HEAD_DIM))
        outs["vp"].append(a["v32"].reshape(B, S, SB_KV_HEADS, SB_HEAD_DIM))
        outs["cp"].append(a["ckv32"])
        outs["rp"].append(a["kr32"])
        outs["up"].append(a["pst"][:, 16 - POOL_STATE_LEN:])
        a = a_s
        outs["ks"].append(a["k32"].reshape(DB, 1, SB_KV_HEADS, SB_HEAD_DIM))
        outs["vs"].append(a["v32"].reshape(DB, 1, SB_KV_HEADS, SB_HEAD_DIM))
        outs["cs"].append(a["ckv32"].reshape(DB, 1, MLA_KV_RANK))
        outs["rs"].append(a["kr32"].reshape(DB, 1, MLA_ROPE_DIM))
        outs["us"].append(jnp.concatenate([state_pool[l][:, 1:], a["pst"].reshape(DB, 1, POOL_WIDTH)], axis=1))
    st = lambda n: jnp.stack(outs[n])
    return (xp, xs, st("kp"), st("vp"), st("cp"), st("rp"), st("up"),
            st("ks"), st("vs"), st("cs"), st("rs"), st("us"))
```

```python
import functools
import math

import jax
import jax.numpy as jnp
from jax import lax
from jax.experimental import pallas as pl
from jax.experimental.pallas import tpu as pltpu

F32 = jnp.float32
BF16 = jnp.bfloat16

D_MODEL = 1024
PAGE_SIZE = 128
SB_HEADS = 8
SB_KV_HEADS = 2
SB_HEAD_DIM = 64
SB_GROUP = SB_HEADS // SB_KV_HEADS
SB_WIDTH = SB_HEADS * SB_HEAD_DIM
MLA_HEADS = 8
MLA_Q_RANK = 256
MLA_KV_RANK = 128
MLA_NOPE_DIM = 64
MLA_ROPE_DIM = 32
MLA_V_DIM = 64
MLA_WIDTH = MLA_HEADS * MLA_V_DIM
ROPE_THETA = 10000.0
POOL_WINDOWS = (2, 4, 8, 16)
POOL_WIDTH = 512
POOL_GROUP_DIM = 128
POOL_STATE_LEN = 15
RMS_EPS = 1e-6
LOG2E = math.log2(math.e)

LANES = 128
SLOT = 128
MLA_QW = 256
VMEM_LIMIT = 56 * 1024 * 1024
ROW_TILE = 512
KEY_TILE = 256
SB_Q_TILE = 512
MLA_TILE = 512

SB_SLOT_HEADS = tuple((s // 2) + SB_GROUP * (s % 2) for s in range(SB_HEADS))

C_BRG = 0
C_SBQ = C_BRG + 3 * D_MODEL
C_GSB = C_SBQ + SB_HEADS * SLOT
C_GMLA = C_GSB + SB_WIDTH
C_PU = C_GMLA + MLA_WIDTH
C_PG = C_PU + POOL_WIDTH
C_CQ = C_PG + POOL_WIDTH
C_K = C_CQ + MLA_Q_RANK
C_V = C_K + LANES
C_CKV = C_V + LANES
C_R = C_CKV + LANES
C_RS = C_R + LANES
IN_PAD = C_RS + LANES


def _dot(a, b):
    return jnp.dot(a, b, preferred_element_type=F32)


def _dot_nt(a, b):
    return lax.dot_general(a, b, (((1,), (1,)), ((), ())), preferred_element_type=F32)


def _sigmoid(x):
    return 1.0 / (1.0 + jnp.exp(-x))


def _rms(x, g):
    return x * lax.rsqrt(jnp.mean(x * x, axis=-1, keepdims=True) + RMS_EPS) * g


def _const_spec(shape):
    nd = len(shape)
    return pl.BlockSpec(shape, lambda *_: (0,) * nd, pipeline_mode=pl.Buffered(1))


def _inproj_kernel(*refs, tm, decode, decode_count):
    (x_ref, npre_ref, win_ref, qn_ref, wuqn_ref, wukbd_ref, wuqr_ref, wuqrs_ref, kvn_ref,
     cos_ref, sin_ref, poolw_ref, pscale_ref) = refs[:13]
    refs = refs[13:]
    if decode:
        state_ref, refs = refs[0], refs[1:]
    (k32_ref, v32_ref, ckv32_ref, kr32_ref, pst_ref, sbq_ref, kb_ref, vb_ref, qcat_ref, kcat_ref,
     gsb_ref, gmla_ref, opool_ref, brg_ref) = refs[:14]
    refs = refs[14:]

    x = x_ref[0]
    h = _rms(x, npre_ref[...]).astype(BF16)

    def proj(off, width):
        return _dot(h, win_ref[:, off:off + width])

    for c in range(0, 3 * D_MODEL, 512):
        brg_ref[0, :, c:c + 512] = _sigmoid(proj(C_BRG + c, 512)).astype(BF16)

    for c in range(0, SB_HEADS * SLOT, 512):
        sbq_ref[0, :, c:c + 512] = (proj(C_SBQ + c, 512) * (LOG2E / math.sqrt(SB_HEAD_DIM))).astype(BF16)
    k = proj(C_K, LANES)
    v = proj(C_V, LANES)
    k32_ref[0] = k
    v32_ref[0] = v
    kb_ref[0] = k.astype(BF16)
    vb_ref[0] = v.astype(BF16)
    g = proj(C_GSB, SB_WIDTH)
    gsb_ref[0] = (g * _sigmoid(g)).astype(BF16)
    g = proj(C_GMLA, MLA_WIDTH)
    gmla_ref[0] = (g * _sigmoid(g)).astype(BF16)

    cos = cos_ref[...]
    sin = sin_ref[...]
    ckv = _rms(proj(C_CKV, LANES), kvn_ref[...])
    krot = proj(C_R, LANES) * cos + proj(C_RS, LANES) * sin
    ckv32_ref[0] = ckv
    kr32_ref[0] = krot[:, :MLA_ROPE_DIM]
    kcat_ref[0, :, :LANES] = ckv.astype(BF16)
    kcat_ref[0, :, LANES:] = krot.astype(BF16)

    cqn = _rms(proj(C_CQ, MLA_Q_RANK), qn_ref[...]).astype(BF16)
    qnope = _dot(cqn, wuqn_ref[...]).astype(BF16)
    qscale = LOG2E / math.sqrt(MLA_NOPE_DIM + MLA_ROPE_DIM)
    for hd in range(MLA_HEADS):
        lo = hd * LANES
        qlat = _dot(qnope, wukbd_ref[:, lo:lo + LANES])
        qrot = _dot(cqn, wuqr_ref[:, lo:lo + LANES]) * cos + _dot(cqn, wuqrs_ref[:, lo:lo + LANES]) * sin
        qcat_ref[0, :, hd * MLA_QW:hd * MLA_QW + LANES] = (qlat * qscale).astype(BF16)
        qcat_ref[0, :, hd * MLA_QW + LANES:(hd + 1) * MLA_QW] = (qrot * qscale).astype(BF16)

    u = proj(C_PU, POOL_WIDTH)
    sums = []
    if decode:
        pst_ref[0] = u
        run = u
        nxt = POOL_STATE_LEN - 1
        for gi, w in enumerate(POOL_WINDOWS):
            while nxt >= POOL_STATE_LEN - (w - 1):
                run = run + state_ref[nxt]
                nxt -= 1
            sums.append(run[:, gi * LANES:(gi + 1) * LANES] * (1.0 / decode_count[gi]))
    else:
        prev_ref = refs[0]
        i = pl.program_id(1)

        @pl.when(i == 0)
        def _():
            prev_ref[...] = jnp.zeros_like(prev_ref)

        ext = jnp.concatenate([prev_ref[...], u], axis=0)
        prev_ref[...] = u[tm - 16:, :]
        pst_ref[0] = u[tm - 16:, :]
        pos = i * tm + lax.broadcasted_iota(jnp.int32, (tm, 1), 0)
        run = ext
        span = 1
        for gi, w in enumerate(POOL_WINDOWS):
            while span < w:
                run = run + pltpu.roll(run, span, 0)
                span *= 2
            cnt = jnp.minimum(pos + 1, w).astype(F32)
            sums.append(run[16:, gi * LANES:(gi + 1) * LANES] / cnt)
    pg = proj(C_PG, POOL_WIDTH)
    pgate = pg * _sigmoid(pg)
    for gi in range(len(POOL_WINDOWS)):
        sl = slice(gi * LANES, (gi + 1) * LANES)
        d = (sums[gi] - u[:, sl]).astype(BF16)
        y = _dot(d, poolw_ref[gi]) * pscale_ref[:, sl]
        opool_ref[0, :, sl] = (y * pgate[:, sl]).astype(BF16)


def _inproj(x, lw, cos, sin, *, tm, state_t=None, decode_count=None):
    B, S, _ = x.shape
    decode = state_t is not None
    nt = S // tm
    row = lambda w: pl.BlockSpec((1, tm, w), lambda b, i: (b, i, 0))
    in_specs = [
        row(D_MODEL),
        _const_spec((1, D_MODEL)),
        _const_spec((D_MODEL, IN_PAD)),
        _const_spec((1, MLA_Q_RANK)),
        _const_spec((MLA_Q_RANK, MLA_HEADS * MLA_NOPE_DIM)),
        _const_spec((MLA_HEADS * MLA_NOPE_DIM, MLA_HEADS * LANES)),
        _const_spec((MLA_Q_RANK, MLA_HEADS * LANES)),
        _const_spec((MLA_Q_RANK, MLA_HEADS * LANES)),
        _const_spec((1, MLA_KV_RANK)),
        pl.BlockSpec((tm, LANES), lambda b, i: (i, 0)),
        pl.BlockSpec((tm, LANES), lambda b, i: (i, 0)),
        _const_spec((len(POOL_WINDOWS), POOL_GROUP_DIM, POOL_GROUP_DIM)),
        _const_spec((1, POOL_WIDTH)),
    ]
    args = [x, lw["npre"], lw["win"], lw["qn"], lw["wuqn"], lw["wukbd"], lw["wuqr"], lw["wuqrs"], lw["kvn"],
            cos, sin, lw["poolw"], lw["pscale"]]
    if decode:
        in_specs.append(_const_spec(state_t.shape))
        args.append(state_t)
        pst_shape, pst_spec = (B, S, POOL_WIDTH), row(POOL_WIDTH)
        scratch = []
    else:
        pst_shape = (B, 16, POOL_WIDTH)
        pst_spec = pl.BlockSpec((1, 16, POOL_WIDTH), lambda b, i: (b, 0, 0))
        scratch = [pltpu.VMEM((16, POOL_WIDTH), F32)]
    outs = [("k32", LANES, F32), ("v32", LANES, F32), ("ckv32", LANES, F32), ("kr32", MLA_ROPE_DIM, F32),
            ("pst", None, F32), ("sbq", SB_HEADS * SLOT, BF16), ("kb", LANES, BF16), ("vb", LANES, BF16),
            ("qcat", MLA_HEADS * MLA_QW, BF16), ("kcat", 2 * LANES, BF16), ("gsb", SB_WIDTH, BF16),
            ("gmla", MLA_WIDTH, BF16), ("opool", POOL_WIDTH, BF16), ("brg", 3 * D_MODEL, BF16)]
    out_shape, out_specs = [], []
    for name, w, dt in outs:
        if name == "pst":
            out_shape.append(jax.ShapeDtypeStruct(pst_shape, dt))
            out_specs.append(pst_spec)
        else:
            out_shape.append(jax.ShapeDtypeStruct((B, S, w), dt))
            out_specs.append(row(w))
    res = pl.pallas_call(
        functools.partial(_inproj_kernel, tm=tm, decode=decode, decode_count=decode_count),
        grid=(B, nt),
        in_specs=in_specs,
        out_specs=out_specs,
        out_shape=out_shape,
        scratch_shapes=scratch,
        compiler_params=pltpu.CompilerParams(dimension_semantics=("arbitrary", "arbitrary"),
                                             vmem_limit_bytes=VMEM_LIMIT),
        name="inproj_decode" if decode else "inproj_prompt",
    )(*args)
    return dict(zip([o[0] for o in outs], res))


def _softplus2(z):
    return jnp.maximum(z, 0.0) + jnp.log(1.0 + jnp.exp2(-jnp.abs(z))) * LOG2E


def _suffix_sums(x, u_tri):
    return _dot(x.astype(BF16), u_tri)


def _sb_weights(z, u_tri, carry, valid=None):
    sp = _softplus2(z)
    if valid is not None:
        sp = jnp.where(valid, sp, 0.0)
    incl = _suffix_sums(sp, u_tri)
    w = jnp.exp2(z - incl - carry)
    if valid is not None:
        w = jnp.where(valid, w, 0.0)
    return w.astype(BF16), carry + incl[:, 0:1]


def _sb_block(q, kblk, vblk, u_tri, carry, valid=None):
    w, carry = _sb_weights(_dot_nt(q, kblk), u_tri, carry, valid)
    return _dot(w, vblk), carry


def _sb_prompt_body(q_ref, k_ref, v_ref, u_tri, o_ref, i, *, tq, tk):
    nsub = tq // tk
    q = jnp.concatenate([q_ref[0, :, :SLOT], q_ref[0, :, SLOT:]], axis=0)
    rows = 2 * tq
    qpos = lax.broadcasted_iota(jnp.int32, (rows, tk), 0) & (tq - 1)
    kpos = lax.broadcasted_iota(jnp.int32, (rows, tk), 1)

    def block(start, carry, valid=None):
        start = pl.multiple_of(start, tk)
        return _sb_block(q, k_ref[0, pl.ds(start, tk), :], v_ref[0, pl.ds(start, tk), :], u_tri, carry, valid)

    acc = jnp.zeros((rows, LANES), F32)
    carry = jnp.zeros((rows, 1), F32)
    for sub in reversed(range(nsub)):
        a, carry = block(i * tq + sub * tk, carry, kpos + sub * tk < qpos)
        acc = acc + a

    def walk(nblk, first):
        def body(t, st):
            acc, carry = st
            for n in range(nblk):
                a, carry = block(i * tq - (first(t) + n + 1) * tk, carry)
                acc = acc + a
            return acc, carry
        return body

    odd = i % 2
    st = lax.fori_loop(0, odd, walk(nsub, lambda t: 0), (acc, carry))
    acc, _ = lax.fori_loop(0, i // 2, walk(2 * nsub, lambda t: nsub * (odd + 2 * t)), st)
    lane = lax.broadcasted_iota(jnp.int32, (tq, LANES), 1)
    o_ref[0] = jnp.where(lane < SB_HEAD_DIM, acc[:tq], acc[tq:]).astype(o_ref.dtype)


def _tri(t):
    j = lax.broadcasted_iota(jnp.int32, (t, t), 0)
    k = lax.broadcasted_iota(jnp.int32, (t, t), 1)
    return (j >= k).astype(BF16)


def _softmax_block(q, kc, m, l, acc, valid=None):
    s = _dot_nt(q, kc)
    if valid is not None:
        s = jnp.where(valid, s, -jnp.inf)
    m_new = jnp.maximum(m, jnp.max(s, axis=-1, keepdims=True))
    alpha = jnp.exp2(m - m_new)
    p = jnp.exp2(s - m_new)
    l = alpha * l + jnp.sum(p, axis=-1, keepdims=True)
    acc = alpha * acc + _dot(p.astype(BF16), kc[:, :MLA_KV_RANK])
    return m_new, l, acc


def _mla_prompt_kernel(q_ref, kc_ref, o_ref, *, t):
    i = pl.program_id(1)
    q = jnp.concatenate([q_ref[0, :, hd * MLA_QW:(hd + 1) * MLA_QW] for hd in range(MLA_HEADS)], axis=0)
    rows = MLA_HEADS * t
    qpos = lax.broadcasted_iota(jnp.int32, (rows, t), 0) & (t - 1)
    kpos = lax.broadcasted_iota(jnp.int32, (rows, t), 1)
    start = pl.multiple_of(i * t, t)
    st = _softmax_block(q, kc_ref[0, pl.ds(start, t), :], jnp.full((rows, 1), -jnp.inf, F32),
                        jnp.zeros((rows, 1), F32), jnp.zeros((rows, MLA_KV_RANK), F32), kpos <= qpos)

    def body(j, st):
        s0 = pl.multiple_of(j * t, t)
        return _softmax_block(q, kc_ref[0, pl.ds(s0, t), :], *st)

    m, l, acc = lax.fori_loop(0, i, body, st)
    out = (acc / l).astype(o_ref.dtype)
    for hd in range(MLA_HEADS):
        o_ref[0, :, hd * MLA_KV_RANK:(hd + 1) * MLA_KV_RANK] = out[hd * t:(hd + 1) * t]


def _mla_prompt(qcat, kcat, *, t):
    B, S, _ = qcat.shape
    return pl.pallas_call(
        functools.partial(_mla_prompt_kernel, t=t),
        grid=(B, S // t),
        in_specs=[
            pl.BlockSpec((1, t, MLA_HEADS * MLA_QW), lambda b, i: (b, i, 0)),
            pl.BlockSpec((1, S, 2 * LANES), lambda b, i: (b, 0, 0)),
        ],
        out_specs=pl.BlockSpec((1, t, MLA_HEADS * MLA_KV_RANK), lambda b, i: (b, i, 0)),
        out_shape=jax.ShapeDtypeStruct((B, S, MLA_HEADS * MLA_KV_RANK), BF16),
        compiler_params=pltpu.CompilerParams(dimension_semantics=("arbitrary",) * 2,
                                             vmem_limit_bytes=VMEM_LIMIT),
        name="mla_prompt",
    )(qcat, kcat)


def _merge_kernel(x_ref, sbo_ref, gsb_ref, mlat_ref, gmla_ref, opool_ref, brg_ref,
                  wsb_ref, wuv_ref, wmla_ref, wpool_ref, wout_ref, npost_ref, y_ref):
    o_sb = (sbo_ref[...].astype(F32) * gsb_ref[...].astype(F32)).astype(BF16)
    merged = brg_ref[:, :D_MODEL].astype(F32) * _dot(o_sb, wsb_ref[...])
    mla_o = _dot(mlat_ref[...], wuv_ref[...])
    o_mla = (mla_o * gmla_ref[...].astype(F32)).astype(BF16)
    merged += brg_ref[:, D_MODEL:2 * D_MODEL].astype(F32) * _dot(o_mla, wmla_ref[...])
    merged += brg_ref[:, 2 * D_MODEL:].astype(F32) * _dot(opool_ref[...], wpool_ref[...])
    o = _dot(merged.astype(BF16), wout_ref[...])
    y_ref[...] = x_ref[...] + _rms(o, npost_ref[...])


def _merge(x2, sbo, gsb, mlat, gmla, opool, brg, lw, *, tm):
    M = x2.shape[0]
    row = lambda w: pl.BlockSpec((tm, w), lambda i: (i, 0))
    return pl.pallas_call(
        _merge_kernel,
        grid=(M // tm,),
        in_specs=[row(D_MODEL), row(SB_WIDTH), row(SB_WIDTH), row(MLA_HEADS * MLA_KV_RANK), row(MLA_WIDTH),
                  row(POOL_WIDTH), row(3 * D_MODEL),
                  _const_spec((SB_WIDTH, D_MODEL)), _const_spec((MLA_HEADS * MLA_KV_RANK, MLA_WIDTH)),
                  _const_spec((MLA_WIDTH, D_MODEL)), _const_spec((POOL_WIDTH, D_MODEL)),
                  _const_spec((D_MODEL, D_MODEL)), _const_spec((1, D_MODEL))],
        out_specs=row(D_MODEL),
        out_shape=jax.ShapeDtypeStruct((M, D_MODEL), F32),
        compiler_params=pltpu.CompilerParams(dimension_semantics=("arbitrary",),
                                             vmem_limit_bytes=VMEM_LIMIT),
        name="merge",
    )(x2, sbo, gsb, mlat, gmla, opool, brg, lw["wsb"], lw["wuvbd"], lw["wmla"], lw["wpool"], lw["wout"],
      lw["npost"])


DEC_BLOCK = 256
DEC_PAGES = 16
PAGES_PER_BLOCK = DEC_BLOCK // PAGE_SIZE
DEC_BLOCKS = DEC_PAGES // PAGES_PER_BLOCK


def _paged_chunk(q_sb, q_lat, q_rope, u_tri, kb, vb, cb, rb, state):
    carry, sacc, m, l, macc = state
    blocks = range(DEC_BLOCKS)
    pair = lambda ref, n: ref[PAGES_PER_BLOCK * n:PAGES_PER_BLOCK * (n + 1)].reshape(DEC_BLOCK, LANES).astype(BF16)
    cpairs = [pair(cb, n) for n in blocks]
    z = jnp.concatenate([_dot(q_sb, kb[n].astype(BF16)) for n in blocks], axis=1)
    s = jnp.concatenate([_dot_nt(q_lat, cpairs[n]) + _dot(q_rope, rb[n].astype(BF16)) for n in blocks], axis=1)

    sp = _softplus2(z)
    lanes = [slice(n * DEC_BLOCK, (n + 1) * DEC_BLOCK) for n in blocks]
    incl = _suffix_sums(jnp.concatenate([sp[:, ln] for ln in lanes], axis=0), u_tri)
    for n in reversed(blocks):
        inc = incl[n * SB_HEADS:(n + 1) * SB_HEADS]
        w = jnp.exp2(z[:, lanes[n]] - inc - carry)
        carry = carry + inc[:, 0:1]
        w2 = jnp.concatenate([w[:, :PAGE_SIZE], w[:, PAGE_SIZE:]], axis=0).astype(BF16)
        r = _dot_nt(w2, pair(vb, n))
        sacc = sacc + r[:SB_HEADS, :LANES] + r[SB_HEADS:, LANES:]

    m_new = jnp.maximum(m, jnp.max(s, axis=-1, keepdims=True))
    alpha = jnp.exp2(m - m_new)
    p = jnp.exp2(s - m_new)
    l = alpha * l + jnp.sum(p, axis=-1, keepdims=True)
    p = p.astype(BF16)
    macc = alpha * macc
    for n in blocks:
        macc = macc + _dot(p[:, lanes[n]], cpairs[n])
    return carry, sacc, m_new, l, macc


def _sb_paged_kernel(pt_ref, q_ref, k_ref, v_ref, u_ref, qs_ref, qc_ref, knew_ref, ck_hbm, cv_hbm, cc_hbm, cr_hbm,
                     o_ref, sbo_ref, mlat_ref, k0, v0, c0, r0, k1, v1, c1, r1, sem, *, layer, n_pages, n_steps, tq, tk):
    i = pl.program_id(2)
    step = (pl.program_id(0) * pl.num_programs(1) + pl.program_id(1)) * pl.num_programs(2) + i
    last = n_steps - 1
    u_tri = u_ref[...]
    hbms = (ck_hbm, cv_hbm, cc_hbm, cr_hbm)
    slots = ((k0, v0, c0, r0), (k1, v1, c1, r1))

    def page_copies(page, p, sl):
        kb, vb, cb, rb = slots[sl]
        half = pl.ds((p % PAGES_PER_BLOCK) * PAGE_SIZE, PAGE_SIZE)
        dsts = (kb.at[p // PAGES_PER_BLOCK, :, half], vb.at[p], cb.at[p], rb.at[p // PAGES_PER_BLOCK, :, half])
        return [pltpu.make_async_copy(hbm.at[layer, page], dst, sem.at[a, sl])
                for a, (hbm, dst) in enumerate(zip(hbms, dsts))]

    def issue(sample, sl):
        for p in range(n_pages):
            for cp in page_copies(pt_ref[sample, p], p, sl):
                cp.start()

    def wait(sl):
        for p in range(n_pages):
            for cp in page_copies(0, p, sl):
                cp.wait()

    def step_body(cur):
        issue(jnp.minimum(step + 1, last), 1 - cur)
        _sb_prompt_body(q_ref, k_ref, v_ref, u_tri, o_ref, i, tq=tq, tk=tk)
        wait(cur)
        kb, vb, cb, rb = slots[cur]
        q_sb = qs_ref[0]
        q_lat = qc_ref[0, :, :MLA_KV_RANK]
        q_rope = qc_ref[0, :, MLA_KV_RANK:MLA_KV_RANK + MLA_ROPE_DIM]
        kn = knew_ref[0].astype(F32)
        s_new = jnp.sum(qc_ref[0].astype(F32) * kn, axis=-1, keepdims=True)
        state = (jnp.zeros((SB_HEADS, 1), F32), jnp.zeros((SB_HEADS, LANES), F32), s_new,
                 jnp.ones((MLA_HEADS, 1), F32), jnp.broadcast_to(kn[:, :MLA_KV_RANK], (MLA_HEADS, MLA_KV_RANK)))
        for ch in reversed(range(n_pages // DEC_PAGES)):
            blocks = pl.ds(ch * DEC_BLOCKS, DEC_BLOCKS)
            pages = pl.ds(ch * DEC_PAGES, DEC_PAGES)
            state = _paged_chunk(q_sb, q_lat, q_rope, u_tri, kb.at[blocks], vb.at[pages], cb.at[pages],
                                 rb.at[blocks], state)
        _, sacc, _, l, macc = state
        lane = lax.broadcasted_iota(jnp.int32, (SB_GROUP, LANES), 1)
        sbo_ref[0] = jnp.where(lane < SB_HEAD_DIM, sacc[:SB_GROUP], sacc[SB_GROUP:]).astype(sbo_ref.dtype)
        mlat_ref[0] = (macc / l).astype(mlat_ref.dtype)

    @pl.when(step == 0)
    def _():
        issue(0, 0)

    for cur in range(2):
        pl.when(step % 2 == cur)(functools.partial(step_body, cur))

    @pl.when(step == last)
    def _():
        wait(1 - last % 2)


def _sb_and_paged(sbq, kb, vb, page_table, q_sb, qcat, knew, ck, cv, cc, cr, *, layer, tq):
    B, S, _ = sbq.shape
    DB, n_pages = page_table.shape
    nq = S // tq
    assert DB == B * SB_GROUP * nq and n_pages % DEC_PAGES == 0 and KEY_TILE == DEC_BLOCK
    n_blocks = n_pages // PAGES_PER_BLOCK
    step = lambda b, p, i: (b * SB_GROUP + p) * nq + i
    grid_spec = pltpu.PrefetchScalarGridSpec(
        num_scalar_prefetch=1,
        grid=(B, SB_GROUP, nq),
        in_specs=[
            pl.BlockSpec((1, tq, 2 * SLOT), lambda b, p, i, pt: (b, i, p)),
            pl.BlockSpec((1, S, LANES), lambda b, p, i, pt: (b, 0, 0)),
            pl.BlockSpec((1, S, LANES), lambda b, p, i, pt: (b, 0, 0)),
            pl.BlockSpec((KEY_TILE, KEY_TILE), lambda b, p, i, pt: (0, 0)),
            pl.BlockSpec((1, SB_HEADS, SLOT), lambda b, p, i, pt: (step(b, p, i), 0, 0)),
            pl.BlockSpec((1, MLA_HEADS, MLA_QW), lambda b, p, i, pt: (step(b, p, i), 0, 0)),
            pl.BlockSpec((1, 1, 2 * LANES), lambda b, p, i, pt: (step(b, p, i), 0, 0)),
            pl.BlockSpec(memory_space=pl.ANY),
            pl.BlockSpec(memory_space=pl.ANY),
            pl.BlockSpec(memory_space=pl.ANY),
            pl.BlockSpec(memory_space=pl.ANY),
        ],
        out_specs=[
            pl.BlockSpec((1, tq, LANES), lambda b, p, i, pt: (b, i, p)),
            pl.BlockSpec((1, SB_GROUP, LANES), lambda b, p, i, pt: (step(b, p, i), 0, 0)),
            pl.BlockSpec((1, MLA_HEADS, MLA_KV_RANK), lambda b, p, i, pt: (step(b, p, i), 0, 0)),
        ],
        scratch_shapes=2 * [
            pltpu.VMEM((n_blocks, LANES, DEC_BLOCK), F32),
            pltpu.VMEM((n_pages, LANES, PAGE_SIZE), F32),
            pltpu.VMEM((n_pages, PAGE_SIZE, MLA_KV_RANK), F32),
            pltpu.VMEM((n_blocks, MLA_ROPE_DIM, DEC_BLOCK), F32),
        ] + [pltpu.SemaphoreType.DMA((4, 2))],
    )
    return pl.pallas_call(
        functools.partial(_sb_paged_kernel, layer=layer, n_pages=n_pages, n_steps=DB, tq=tq, tk=KEY_TILE),
        grid_spec=grid_spec,
        out_shape=[jax.ShapeDtypeStruct((B, S, SB_WIDTH), BF16),
                   jax.ShapeDtypeStruct((DB, SB_GROUP, LANES), BF16),
                   jax.ShapeDtypeStruct((DB, MLA_HEADS, MLA_KV_RANK), BF16)],
        compiler_params=pltpu.CompilerParams(dimension_semantics=("arbitrary",) * 3,
                                             vmem_limit_bytes=VMEM_LIMIT),
        name="sb_prompt_paged",
    )(page_table, sbq, kb, vb, _tri(KEY_TILE), q_sb, qcat, knew, ck, cv, cc, cr)


def _block_diag(blocks):
    n = len(blocks)
    r, c = blocks[0].shape
    rows = []
    for i, blk in enumerate(blocks):
        rows.append(jnp.concatenate([jnp.zeros((r, c * i), blk.dtype), blk,
                                     jnp.zeros((r, c * (n - 1 - i)), blk.dtype)], axis=1))
    return jnp.concatenate(rows, axis=0)


def _rope_cols(w):
    half = MLA_ROPE_DIM // 2
    z = jnp.zeros((w.shape[0], LANES - MLA_ROPE_DIM), w.dtype)
    return jnp.concatenate([w, z], axis=1), jnp.concatenate([-w[:, half:], w[:, :half], z], axis=1)


def _prep_layer(l, norm_pre, norm_post, w_in, mla_q_norm, mla_kv_norm, mla_w_uq, mla_w_uk, mla_w_uv,
                pool_w, pool_scale, w_br_sb, w_br_mla, w_br_pool, w_out):
    w = w_in[l].astype(BF16)
    o = 0
    parts = {}
    for name, width in (("sbq", SB_WIDTH), ("k", LANES), ("v", LANES), ("gsb", SB_WIDTH), ("cq", MLA_Q_RANK),
                        ("ckv", MLA_KV_RANK), ("r", MLA_ROPE_DIM), ("gmla", MLA_WIDTH), ("pu", POOL_WIDTH),
                        ("pg", POOL_WIDTH), ("brg", 3 * D_MODEL)):
        parts[name] = w[:, o:o + width]
        o += width
    z64 = jnp.zeros((D_MODEL, SB_HEAD_DIM), BF16)
    sbq_cols, gsb_cols = [], []
    for s, hd in enumerate(SB_SLOT_HEADS):
        qh = parts["sbq"][:, hd * SB_HEAD_DIM:(hd + 1) * SB_HEAD_DIM]
        sbq_cols += [qh, z64] if s % 2 == 0 else [z64, qh]
        gsb_cols.append(parts["gsb"][:, hd * SB_HEAD_DIM:(hd + 1) * SB_HEAD_DIM])
    r_cols, rs_cols = _rope_cols(parts["r"])
    win = jnp.concatenate([parts["brg"]] + sbq_cols + gsb_cols +
                          [parts["gmla"], parts["pu"], parts["pg"], parts["cq"], parts["k"], parts["v"],
                           parts["ckv"], r_cols, rs_cols], axis=1)

    wuq = mla_w_uq[l].astype(BF16).reshape(MLA_Q_RANK, MLA_HEADS, MLA_NOPE_DIM + MLA_ROPE_DIM)
    wuqn = wuq[:, :, :MLA_NOPE_DIM].reshape(MLA_Q_RANK, MLA_HEADS * MLA_NOPE_DIM)
    rr = [_rope_cols(wuq[:, hd, MLA_NOPE_DIM:]) for hd in range(MLA_HEADS)]
    wuqr = jnp.concatenate([a for a, _ in rr], axis=1)
    wuqrs = jnp.concatenate([b for _, b in rr], axis=1)
    wuk = mla_w_uk[l].astype(BF16)
    wukbd = _block_diag([wuk[:, hd, :].T for hd in range(MLA_HEADS)])
    wuv = mla_w_uv[l].astype(BF16)
    wuvbd = _block_diag([wuv[:, hd, :] for hd in range(MLA_HEADS)])
    wsb = w_br_sb[l].astype(BF16)
    wsb = jnp.concatenate([wsb[hd * SB_HEAD_DIM:(hd + 1) * SB_HEAD_DIM] for hd in SB_SLOT_HEADS], axis=0)
    return dict(
        npre=norm_pre[l].reshape(1, D_MODEL), npost=norm_post[l].reshape(1, D_MODEL), win=win,
        qn=mla_q_norm[l].reshape(1, MLA_Q_RANK), kvn=mla_kv_norm[l].reshape(1, MLA_KV_RANK),
        wuqn=wuqn, wukbd=wukbd, wuqr=wuqr, wuqrs=wuqrs, wuvbd=wuvbd,
        poolw=pool_w[l].astype(BF16), pscale=pool_scale[l].reshape(1, POOL_WIDTH),
        wsb=wsb, wmla=w_br_mla[l].astype(BF16), wpool=w_br_pool[l].astype(BF16), wout=w_out[l].astype(BF16))


def _rope_tables(pos):
    half = MLA_ROPE_DIM // 2
    inv = ROPE_THETA ** (-jnp.arange(half, dtype=F32) / half)
    ang = pos.astype(F32)[:, None] * inv[None, :]
    z = jnp.zeros((pos.shape[0], LANES - MLA_ROPE_DIM), F32)
    cos, sin = jnp.cos(ang), jnp.sin(ang)
    return jnp.concatenate([cos, cos, z], axis=1), jnp.concatenate([sin, sin, z], axis=1)


def _layer(xp, xs, lw, rope_p, rope_s, state, page_table, ck, cv, cc, cr, layer, past):
    B, S, _ = xp.shape
    DB = xs.shape[0]
    tm = min(ROW_TILE, S)
    count = tuple(float(min(past + 1, w)) for w in POOL_WINDOWS)
    ap = _inproj(xp, lw, *rope_p, tm=tm)
    as_ = _inproj(xs.reshape(1, DB, D_MODEL), lw, *rope_s, tm=DB, state_t=jnp.transpose(state, (1, 0, 2)),
                  decode_count=count)
    q_sb = as_["sbq"].reshape(DB, SB_GROUP, 2, SLOT).transpose(0, 2, 1, 3).reshape(DB, SB_HEADS, SLOT)
    sbo_p, sbo_s, mlat_s = _sb_and_paged(ap["sbq"], ap["kb"], ap["vb"], page_table, q_sb,
                                         as_["qcat"].reshape(DB, MLA_HEADS, MLA_QW),
                                         as_["kcat"].reshape(DB, 1, 2 * LANES), ck, cv, cc, cr,
                                         layer=layer, tq=min(SB_Q_TILE, S))
    mlat_p = _mla_prompt(ap["qcat"], ap["kcat"], t=min(MLA_TILE, S))
    flat = lambda v: v.reshape(-1, v.shape[-1])
    yp = _merge(flat(xp), flat(sbo_p), flat(ap["gsb"]), flat(mlat_p), flat(ap["gmla"]), flat(ap["opool"]),
                flat(ap["brg"]), lw, tm=tm)
    ys = _merge(flat(xs), sbo_s.reshape(DB, SB_WIDTH), flat(as_["gsb"]), mlat_s.reshape(DB, -1),
                flat(as_["gmla"]), flat(as_["opool"]), flat(as_["brg"]), lw, tm=DB)
    return yp.reshape(B, S, D_MODEL), ys.reshape(DB, 1, D_MODEL), ap, as_


def kernel(x_prompt, x_sample, cache_sb_k, cache_sb_v, cache_mla_ckv, cache_mla_krope, state_pool, page_table,
           norm_pre, norm_post, w_in, mla_q_norm, mla_kv_norm, mla_w_uq, mla_w_uk, mla_w_uv,
           pool_w, pool_scale, w_br_sb, w_br_mla, w_br_pool, w_out):
    B, S, _ = x_prompt.shape
    DB, T, _ = x_sample.shape
    assert T == 1
    depth = w_in.shape[0]
    n_pages = page_table.shape[1]
    past = n_pages * PAGE_SIZE
    n_pool = cache_sb_k.shape[1]
    ck = jnp.transpose(cache_sb_k, (0, 1, 3, 4, 2)).reshape(depth, n_pool, LANES, PAGE_SIZE)
    cv = jnp.transpose(cache_sb_v, (0, 1, 3, 4, 2)).reshape(depth, n_pool, LANES, PAGE_SIZE)
    cr = jnp.transpose(cache_mla_krope, (0, 1, 3, 2))

    cos_p, sin_p = _rope_tables(jnp.arange(S, dtype=jnp.int32))
    cos_s, sin_s = _rope_tables(jnp.full((DB,), past, dtype=jnp.int32))

    xp, xs = x_prompt, x_sample
    outs = {n: [] for n in ("kp", "vp", "cp", "rp", "up", "ks", "vs", "cs", "rs", "us")}
    for l in range(depth):
        lw = _prep_layer(l, norm_pre, norm_post, w_in, mla_q_norm, mla_kv_norm, mla_w_uq, mla_w_uk, mla_w_uv,
                         pool_w, pool_scale, w_br_sb, w_br_mla, w_br_pool, w_out)
        xp, xs, a, a_s = _layer(xp, xs, lw, (cos_p, sin_p), (cos_s, sin_s), state_pool[l], page_table,
                                ck, cv, cache_mla_ckv, cr, l, past)
        outs["kp"].append(a["k32"].reshape(B, S, SB_KV_HEADS, SB_HEAD_DIM))
        outs["vp"].append(a["v32"].reshape(B, S, SB_KV_HEADS, SB_HEAD_DIM))
        outs["cp"].append(a["ckv32"])
        outs["rp"].append(a["kr32"])
        outs["up"].append(a["pst"][:, 16 - POOL_STATE_LEN:])
        a = a_s
        outs["ks"].append(a["k32"].reshape(DB, 1, SB_KV_HEADS, SB_HEAD_DIM))
        outs["vs"].append(a["v32"].reshape(DB, 1, SB_KV_HEADS, SB_HEAD_DIM))
        outs["cs"].append(a["ckv32"].reshape(DB, 1, MLA_KV_RANK))
        outs["rs"].append(a["kr32"].reshape(DB, 1, MLA_ROPE_DIM))
        outs["us"].append(jnp.concatenate([state_pool[l][:, 1:], a["pst"].reshape(DB, 1, POOL_WIDTH)], axis=1))
    st = lambda n: jnp.stack(outs[n])
    return (xp, xs, st("kp"), st("vp"), st("cp"), st("rp"), st("up"),
            st("ks"), st("vs"), st("cs"), st("rs"), st("us"))
```
